```python
import functools
import jax, jax.numpy as jnp
from jax import lax
import numpy as np

D_MODEL = 1024
BATCH = 8
SEQ = 2048
DEPTH = 4
DEC_BATCH = 128
DEC_SEQ = 8
PAST_LEN = 8192
PAGE_SIZE = 128

N_HEADS = 8
QK_NOPE = 64
QK_ROPE = 32
V_HEAD = 64
Q_LORA = 256
KV_LORA = 256
ROPE_THETA = 10000.0
SM_SCALE = (QK_NOPE + QK_ROPE) ** -0.5
Q_BLOCK = 128
CONV_CH = D_MODEL // 2
CONV_K = 3
MLA_W = N_HEADS * V_HEAD
MIX_W = MLA_W + CONV_CH
IN_SPLITS = (Q_LORA, Q_LORA + KV_LORA, Q_LORA + KV_LORA + QK_ROPE, Q_LORA + KV_LORA + QK_ROPE + CONV_CH, Q_LORA + KV_LORA + QK_ROPE + 2 * CONV_CH)
IN_COLS = Q_LORA + KV_LORA + QK_ROPE + 3 * CONV_CH
N_EXPERTS = 16
N_GROUPS = 4
EXP_PER_GROUP = N_EXPERTS // N_GROUPS
TOP_K = 2
D_EXPERT = D_MODEL // 2
ALPHA = (2 * DEPTH) ** 0.25
BETA = (8 * DEPTH) ** -0.25
LN_EPS = 1e-5
RMS_EPS = 1e-6
ADA_INIT = 0.3

kernel_name = "hymba_mla_shortconv_groupmoe_deepnorm_adaln_step"


def layer_norm(x, g, b):
    xf = x.astype(jnp.float32)
    mu = xf.mean(-1, keepdims=True)
    var = jnp.square(xf - mu).mean(-1, keepdims=True)
    return ((xf - mu) * lax.rsqrt(var + LN_EPS) * g + b).astype(x.dtype)


def rms_norm(x, g):
    xf = x.astype(jnp.float32)
    return (xf * lax.rsqrt(jnp.square(xf).mean(-1, keepdims=True) + RMS_EPS) * g).astype(x.dtype)


def rope(x, pos):
    half = x.shape[-1] // 2
    inv = jnp.power(ROPE_THETA, -jnp.arange(half, dtype=jnp.float32) * 2.0 / x.shape[-1])
    ang = pos.astype(jnp.float32)[:, None] * inv[None, :]
    ang = ang.reshape((1, pos.shape[0]) + (1,) * (x.ndim - 3) + (half,))
    cos, sin = jnp.cos(ang), jnp.sin(ang)
    xf = x.astype(jnp.float32)
    x1, x2 = xf[..., :half], xf[..., half:]
    return jnp.concatenate([x1 * cos - x2 * sin, x2 * cos + x1 * sin], -1).astype(x.dtype)


def mla_attend(q_lat, q_pe, kv_lat, k_pe, q_pos, k_pos):
    s = (jnp.einsum("bthr,bsr->bhts", q_lat, kv_lat) + jnp.einsum("bthp,bsp->bhts", q_pe, k_pe)).astype(jnp.float32) * SM_SCALE
    mask = k_pos[None, :] <= q_pos[:, None]
    s = jnp.where(mask, s, jnp.finfo(jnp.float32).min)
    p = jax.nn.softmax(s, axis=-1).astype(kv_lat.dtype)
    return jnp.einsum("bhts,bsr->bthr", p, kv_lat)


def prompt_attention(q_lat, q_pe, ckv, k_pe):
    b, s = q_lat.shape[:2]
    nb = s // Q_BLOCK

    def blocks(a):
        return a.reshape((b, nb, Q_BLOCK) + a.shape[2:]).swapaxes(0, 1)

    k_pos = jnp.arange(s)

    def one(args):
        i, ql, qp = args
        return mla_attend(ql, qp, ckv, k_pe, i * Q_BLOCK + jnp.arange(Q_BLOCK), k_pos)

    o = lax.map(one, (jnp.arange(nb), blocks(q_lat), blocks(q_pe)))
    return o.swapaxes(0, 1).reshape(q_lat.shape)


def sample_attention(q_lat, q_pe, ckv, k_pe, past_lat, past_kpe):
    past_len = past_lat.shape[1]
    t = q_lat.shape[1]
    kv = jnp.concatenate([past_lat.astype(ckv.dtype), ckv], axis=1)
    kp = jnp.concatenate([past_kpe.astype(k_pe.dtype), k_pe], axis=1)
    return mla_attend(q_lat, q_pe, kv, kp, past_len + jnp.arange(t), jnp.arange(past_len + t))


def mixer_inputs(h, pos, w_in, q_norm_g, w_uq, kv_norm_g, w_uk):
    z = h @ w_in
    q_a, kv_a, k_r, gate_b, gate_c, v = jnp.split(z, IN_SPLITS, axis=-1)
    q = jnp.einsum("btr,rhn->bthn", rms_norm(q_a, q_norm_g), w_uq)
    q_lat = jnp.einsum("bthn,rhn->bthr", q[..., :QK_NOPE], w_uk)
    q_pe = rope(q[..., QK_NOPE:], pos)
    ckv = rms_norm(kv_a, kv_norm_g)
    k_pe = rope(k_r, pos)
    return q_lat, q_pe, ckv, k_pe, gate_b, gate_c * v


def short_conv(ext, conv_w):
    t = ext.shape[1] - (CONV_K - 1)
    y = ext[:, 0:t] * conv_w[0]
    for k in range(1, CONV_K):
        y = y + ext[:, k:k + t] * conv_w[k]
    return y


def moe(h, w_router, router_bias, w_gate, w_up, w_down):
    shp = h.shape
    t = h.reshape(-1, shp[-1])
    scores = jax.nn.sigmoid((t @ w_router).astype(jnp.float32))
    sel = (scores + router_bias.astype(jnp.float32)).reshape(-1, N_GROUPS, EXP_PER_GROUP)
    grp_score = lax.top_k(sel, TOP_K)[0].sum(-1)
    g_idx = jnp.argmax(grp_score, axis=-1)
    in_grp = jnp.take_along_axis(sel, g_idx[:, None, None], axis=1)[:, 0]
    _, local = lax.top_k(in_grp, TOP_K)
    eid = g_idx[:, None] * EXP_PER_GROUP + local
    w = jnp.take_along_axis(scores, eid, axis=-1)
    w = w / w.sum(-1, keepdims=True)
    gates = jnp.einsum("nk,nke->ne", w, jax.nn.one_hot(eid, N_EXPERTS, dtype=jnp.float32)).astype(t.dtype)
    hid = jax.nn.silu(jnp.einsum("nd,edf->nef", t, w_gate)) * jnp.einsum("nd,edf->nef", t, w_up)
    out = jnp.einsum("nef,efd->nd", hid * gates[:, :, None], w_down)
    return out.reshape(shp)


def decoder_layer(x, cvec, pos, conv_prefix, attend, w_in, q_norm_g, w_uq, kv_norm_g, w_uk, w_uv, conv_w, w_o,
                  ln1_g, ln1_b, ln2_g, ln2_b, w_ada, b_ada, w_gate, w_up, w_down, w_router, router_bias):
    b, t, _ = x.shape
    mod = (cvec @ w_ada + b_ada)[:, None, :]
    sh1, sc1, g1, sh2, sc2, g2 = jnp.split(mod, 6, axis=-1)
    h = x * (1 + sc1) + sh1
    q_lat, q_pe, ckv, k_pe, gate_b, u = mixer_inputs(h, pos, w_in, q_norm_g, w_uq, kv_norm_g, w_uk)
    o_lat = attend(q_lat, q_pe, ckv, k_pe)
    o_att = jnp.einsum("bthr,rhv->bthv", o_lat, w_uv).reshape(b, t, MLA_W)
    ext = jnp.concatenate([conv_prefix.astype(u.dtype), u], axis=1)
    o_conv = gate_b * short_conv(ext, conv_w)
    mix = jnp.concatenate([o_att, o_conv], axis=-1) @ w_o
    x = layer_norm(ALPHA * x + g1 * mix, ln1_g, ln1_b)
    h = x * (1 + sc2) + sh2
    x = layer_norm(ALPHA * x + g2 * moe(h, w_router, router_bias, w_gate, w_up, w_down), ln2_g, ln2_b)
    return x, ckv, k_pe, ext[:, -(CONV_K - 1):]


def setup_inputs(seed: int = 0) -> dict:
    key = jax.random.key(seed)
    ks = jax.random.split(key, 32)
    f32 = jnp.float32
    nrm = jax.random.normal
    n_pages = PAST_LEN // PAGE_SIZE
    n_used = DEC_BATCH * n_pages
    n_pool = n_used + n_used // 4
    page_table = jax.random.permutation(ks[0], n_pool)[:n_used].reshape(DEC_BATCH, n_pages).astype(jnp.int32)
    return {
        "x_prompt": nrm(ks[1], (BATCH, SEQ, D_MODEL), f32),
        "x_sample": nrm(ks[2], (DEC_BATCH, DEC_SEQ, D_MODEL), f32),
        "cache_latent": nrm(ks[3], (DEPTH, n_pool, PAGE_SIZE, KV_LORA), f32),
        "cache_krope": nrm(ks[4], (DEPTH, n_pool, PAGE_SIZE, QK_ROPE), f32),
        "state_conv": nrm(ks[5], (DEPTH, DEC_BATCH, CONV_K - 1, CONV_CH), f32),
        "page_table": page_table,
        "c_prompt": nrm(ks[6], (BATCH, D_MODEL), f32),
        "c_sample": nrm(ks[7], (DEC_BATCH, D_MODEL), f32),
        "w_in": nrm(ks[8], (DEPTH, D_MODEL, IN_COLS), f32) * D_MODEL ** -0.5,
        "q_norm_g": 1.0 + 0.1 * nrm(ks[9], (DEPTH, Q_LORA), f32),
        "w_uq": nrm(ks[10], (DEPTH, Q_LORA, N_HEADS, QK_NOPE + QK_ROPE), f32) * Q_LORA ** -0.5,
        "kv_norm_g": 1.0 + 0.1 * nrm(ks[11], (DEPTH, KV_LORA), f32),
        "w_uk": nrm(ks[12], (DEPTH, KV_LORA, N_HEADS, QK_NOPE), f32) * KV_LORA ** -0.5,
        "w_uv": nrm(ks[13], (DEPTH, KV_LORA, N_HEADS, V_HEAD), f32) * KV_LORA ** -0.5,
        "conv_w": nrm(ks[14], (DEPTH, CONV_K, CONV_CH), f32) * CONV_K ** -0.5,
        "w_o": nrm(ks[15], (DEPTH, MIX_W, D_MODEL), f32) * (MIX_W ** -0.5 * BETA),
        "ln1_g": 1.0 + 0.1 * nrm(ks[16], (DEPTH, D_MODEL), f32),
        "ln1_b": 0.01 * nrm(ks[17], (DEPTH, D_MODEL), f32),
        "ln2_g": 1.0 + 0.1 * nrm(ks[18], (DEPTH, D_MODEL), f32),
        "ln2_b": 0.01 * nrm(ks[19], (DEPTH, D_MODEL), f32),
        "w_ada": nrm(ks[20], (DEPTH, D_MODEL, 6 * D_MODEL), f32) * (ADA_INIT * D_MODEL ** -0.5),
        "b_ada": 0.01 * nrm(ks[21], (DEPTH, 6 * D_MODEL), f32),
        "w_router": nrm(ks[22], (D_MODEL, N_EXPERTS), f32) * D_MODEL ** -0.5,
        "router_bias": 0.01 * nrm(ks[23], (N_EXPERTS,), f32),
        "w_gate": nrm(ks[24], (DEPTH, N_EXPERTS, D_MODEL, D_EXPERT), f32) * D_MODEL ** -0.5,
        "w_up": nrm(ks[25], (DEPTH, N_EXPERTS, D_MODEL, D_EXPERT), f32) * D_MODEL ** -0.5,
        "w_down": nrm(ks[26], (DEPTH, N_EXPERTS, D_EXPERT, D_MODEL), f32) * (D_EXPERT ** -0.5 * BETA),
    }


def reference(x_prompt, x_sample, cache_latent, cache_krope, state_conv, page_table, c_prompt, c_sample,
              w_in, q_norm_g, w_uq, kv_norm_g, w_uk, w_uv, conv_w, w_o, ln1_g, ln1_b, ln2_g, ln2_b,
              w_ada, b_ada, w_router, router_bias, w_gate, w_up, w_down):
    seq = x_prompt.shape[1]
    dec_b, dec_seq = x_sample.shape[:2]
    past_len = page_table.shape[1] * PAGE_SIZE
    pos_p = jnp.arange(seq)
    pos_s = past_len + jnp.arange(dec_seq)
    xp, xs = x_prompt, x_sample
    lat_p, kpe_p, conv_p, lat_s, kpe_s, conv_s = [], [], [], [], [], []
    for l in range(DEPTH):
        lw = dict(w_in=w_in[l], q_norm_g=q_norm_g[l], w_uq=w_uq[l], kv_norm_g=kv_norm_g[l], w_uk=w_uk[l],
                  w_uv=w_uv[l], conv_w=conv_w[l], w_o=w_o[l], ln1_g=ln1_g[l], ln1_b=ln1_b[l],
                  ln2_g=ln2_g[l], ln2_b=ln2_b[l], w_ada=w_ada[l], b_ada=b_ada[l], w_gate=w_gate[l],
                  w_up=w_up[l], w_down=w_down[l], w_router=w_router, router_bias=router_bias)
        prefix_p = jnp.zeros((xp.shape[0], CONV_K - 1, CONV_CH), xp.dtype)
        xp, ckv, kpe, cst = decoder_layer(xp, c_prompt, pos_p, prefix_p, prompt_attention, **lw)
        lat_p.append(ckv)
        kpe_p.append(kpe)
        conv_p.append(cst)
        past_lat = cache_latent[l, page_table].reshape(dec_b, past_len, KV_LORA)
        past_kpe = cache_krope[l, page_table].reshape(dec_b, past_len, QK_ROPE)
        attend_s = functools.partial(sample_attention, past_lat=past_lat, past_kpe=past_kpe)
        xs, ckv, kpe, cst = decoder_layer(xs, c_sample, pos_s, state_conv[l], attend_s, **lw)
        lat_s.append(ckv)
        kpe_s.append(kpe)
        conv_s.append(cst)
    return (xp, xs, jnp.stack(lat_p), jnp.stack(kpe_p), jnp.stack(conv_p), jnp.stack(lat_s), jnp.stack(kpe_s), jnp.stack(conv_s))
```

```python
import functools

import jax
import jax.numpy as jnp
from jax import lax
from jax.experimental import pallas as pl
from jax.experimental.pallas import tpu as pltpu

N_HEADS = 8
QK_NOPE = 64
QK_ROPE = 32
V_HEAD = 64
Q_LORA = 256
KV_LORA = 256
ROPE_THETA = 10000.0
SM_SCALE = (QK_NOPE + QK_ROPE) ** -0.5
PAGE_SIZE = 128
CONV_K = 3
N_EXPERTS = 16
N_GROUPS = 4
EXP_PER_GROUP = N_EXPERTS // N_GROUPS
LN_EPS = 1e-5
RMS_EPS = 1e-6

LANES = 128
ROPE_PAD = LANES
QK_W = KV_LORA + ROPE_PAD
NEG_BIG = -1e30
VMEM_LIMIT = 56 * 1024 * 1024

BF16 = jnp.bfloat16
F32 = jnp.float32


def _cparams(sem):
    return pltpu.CompilerParams(dimension_semantics=sem, vmem_limit_bytes=VMEM_LIMIT)


def _dot(a, b):
    return jnp.dot(a, b, preferred_element_type=F32)


def _dot_nt(a, b):
    return lax.dot_general(a, b, (((1,), (1,)), ((), ())), preferred_element_type=F32)


def _layer_norm(y, g, b):
    mu = jnp.mean(y, axis=-1, keepdims=True)
    d = y - mu
    var = jnp.mean(d * d, axis=-1, keepdims=True)
    return d * lax.rsqrt(var + LN_EPS) * g + b


def _rms_scale(v):
    return lax.rsqrt(jnp.mean(v * v, axis=-1, keepdims=True) + RMS_EPS)


def _ada_kernel(c_ref, w_ref, b_ref, o_ref):
    o_ref[...] = _dot(c_ref[...], w_ref[...].astype(BF16)) + b_ref[...]


def _ada_mod(c_all, w_ada, b_ada):
    depth, d, d6 = w_ada.shape
    n = c_all.shape[0]
    tn = 512
    return pl.pallas_call(
        _ada_kernel,
        out_shape=jax.ShapeDtypeStruct((depth, n, d6), F32),
        grid=(depth, d6 // tn),
        in_specs=[
            pl.BlockSpec((n, d), lambda l, j: (0, 0)),
            pl.BlockSpec((None, d, tn), lambda l, j: (l, 0, j)),
            pl.BlockSpec((None, 1, tn), lambda l, j: (l, 0, j)),
        ],
        out_specs=pl.BlockSpec((None, n, tn), lambda l, j: (l, 0, j)),
        compiler_params=_cparams(("arbitrary", "arbitrary")),
        name="ada_mod",
    )(c_all.astype(BF16), w_ada, b_ada.reshape(depth, 1, d6))


def _qfuse_kernel(uq_ref, uk_ref, g_ref, o_ref):
    a = uq_ref[...] * g_ref[...] * SM_SCALE
    o_ref[...] = lax.dot_general(a, uk_ref[...], (((1,), (1,)), ((), ())),
                                 preferred_element_type=F32,
                                 precision=lax.Precision.HIGHEST)


def _fuse_q_weights(w_uq_nope_t, w_uk_t, q_norm_g):
    depth = w_uq_nope_t.shape[0]
    return pl.pallas_call(
        _qfuse_kernel,
        out_shape=jax.ShapeDtypeStruct((depth, N_HEADS, Q_LORA, KV_LORA), F32),
        grid=(depth, N_HEADS),
        in_specs=[
            pl.BlockSpec((None, None, Q_LORA, QK_NOPE), lambda l, h: (l, h, 0, 0)),
            pl.BlockSpec((None, None, KV_LORA, QK_NOPE), lambda l, h: (l, h, 0, 0)),
            pl.BlockSpec((None, Q_LORA, 1), lambda l, h: (l, 0, 0)),
        ],
        out_specs=pl.BlockSpec((None, None, Q_LORA, KV_LORA), lambda l, h: (l, h, 0, 0)),
        compiler_params=_cparams(("arbitrary", "arbitrary")),
        name="fuse_q_weights",
    )(w_uq_nope_t, w_uk_t, q_norm_g)


C_QA = 0
C_KV = C_QA + Q_LORA
C_GB = C_KV + KV_LORA
CONV_CH = 512
C_GC = C_GB + CONV_CH
C_V = C_GC + CONV_CH
C_KR = C_V + CONV_CH
C_KRS = C_KR + ROPE_PAD
IN_COLS_R = C_KRS + ROPE_PAD


def _premix_common(x_ref, sh_ref, sc_ref, w_ref, kvg_ref, cw_ref, cos_ref, sin_ref):
    h = (x_ref[...] * (1.0 + sc_ref[...]) + sh_ref[...]).astype(BF16)
    q_a = _dot(h, w_ref[:, C_QA:C_KV])
    qn = (q_a * _rms_scale(q_a)).astype(BF16)
    kv_a = _dot(h, w_ref[:, C_KV:C_GB])
    ckv = kv_a * _rms_scale(kv_a) * kvg_ref[...]
    kr = _dot(h, w_ref[:, C_KR:C_KRS])
    krs = _dot(h, w_ref[:, C_KRS:IN_COLS_R])
    kpe = kr * cos_ref[...] + krs * sin_ref[...]
    gate_b = _dot(h, w_ref[:, C_GB:C_GC])
    u = _dot(h, w_ref[:, C_GC:C_V]) * _dot(h, w_ref[:, C_V:C_KR])
    return qn, ckv, kpe, gate_b, u


def _premix_prompt_kernel(x_ref, sh_ref, sc_ref, w_ref, kvg_ref, cw_ref, cos_ref, sin_ref,
                          qn_ref, ckv_ref, kpe_ref, kcat_ref, oconv_ref, cst_ref,
                          ubuf, *, tiles_per_seq):
    tm = x_ref.shape[0]
    qn, ckv, kpe, gate_b, u = _premix_common(x_ref, sh_ref, sc_ref, w_ref, kvg_ref, cw_ref,
                                             cos_ref, sin_ref)
    qn_ref[...] = qn
    ckv_ref[...] = ckv
    kpe_ref[...] = kpe[:, :QK_ROPE]
    kcat_ref[:, :KV_LORA] = ckv.astype(BF16)
    kcat_ref[:, KV_LORA:] = kpe.astype(BF16)

    @pl.when(pl.program_id(0) % tiles_per_seq == 0)
    def _():
        ubuf[0:8, :] = jnp.zeros((8, CONV_CH), F32)

    ubuf[8:8 + tm, :] = u
    y = (ubuf[6:6 + tm, :] * cw_ref[0:1, :] + ubuf[7:7 + tm, :] * cw_ref[1:2, :]
         + u * cw_ref[2:3, :])
    oconv_ref[...] = (gate_b * y).astype(BF16)
    cst_ref[...] = u[tm - (CONV_K - 1):, :]
    ubuf[0:8, :] = u[tm - 8:, :]


def _premix_sample_kernel(x_ref, sh_ref, sc_ref, w_ref, kvg_ref, cw_ref, cos_ref, sin_ref,
                          e1_ref, e2_ref, wa_ref, wb_ref,
                          qs_ref, ckv_ref, kpe_ref, kcat_ref, oconv_ref, u_ref,
                          ubuf, *, dec_seq):
    tm = x_ref.shape[0]
    qn, ckv, kpe, gate_b, u = _premix_common(x_ref, sh_ref, sc_ref, w_ref, kvg_ref, cw_ref,
                                             cos_ref, sin_ref)
    ckv_ref[...] = ckv
    kpe_ref[...] = kpe[:, :QK_ROPE]
    kcat_ref[:, :KV_LORA] = ckv
    kcat_ref[:, KV_LORA:] = kpe
    u_ref[...] = u

    ubuf[0:8, :] = jnp.zeros((8, CONV_CH), F32)
    ubuf[8:8 + tm, :] = u
    t = lax.broadcasted_iota(jnp.int32, (tm, CONV_CH), 0) % dec_seq
    u1 = jnp.where(t >= 1, ubuf[7:7 + tm, :], e1_ref[...])
    u2 = jnp.where(t >= 2, ubuf[6:6 + tm, :], e2_ref[...])
    y = u2 * cw_ref[0:1, :] + u1 * cw_ref[1:2, :] + u * cw_ref[2:3, :]
    oconv_ref[...] = (gate_b * y).astype(BF16)

    nb = tm // dec_seq
    for h in range(N_HEADS):
        qa = _dot(qn, wa_ref[h])
        qr = qa[:, KV_LORA:] * cos_ref[...] + _dot(qn, wb_ref[h]) * sin_ref[...]
        qs_ref[:, h * dec_seq:(h + 1) * dec_seq, :KV_LORA] = (
            qa[:, :KV_LORA].reshape(nb, dec_seq, KV_LORA))
        qs_ref[:, h * dec_seq:(h + 1) * dec_seq, KV_LORA:] = qr.reshape(nb, dec_seq, ROPE_PAD)


def _premix_prompt(l, x, mod_p, w_in_r, kv_g, conv_w, cos_t, sin_t, batch, seq, tm):
    n, d = x.shape
    tiles_per_seq = seq // tm
    kern = functools.partial(_premix_prompt_kernel, tiles_per_seq=tiles_per_seq)
    row = lambda i: (i, 0)
    outs = pl.pallas_call(
        kern,
        out_shape=(
            jax.ShapeDtypeStruct((n, Q_LORA), BF16),
            jax.ShapeDtypeStruct((n, KV_LORA), F32),
            jax.ShapeDtypeStruct((n, QK_ROPE), F32),
            jax.ShapeDtypeStruct((n, QK_W), BF16),
            jax.ShapeDtypeStruct((n, CONV_CH), BF16),
            jax.ShapeDtypeStruct((batch, CONV_K - 1, CONV_CH), F32),
        ),
        grid=(n // tm,),
        in_specs=[
            pl.BlockSpec((tm, d), row),
            pl.BlockSpec((None, None, 1, d), lambda i: (l, i // tiles_per_seq, 0, 0)),
            pl.BlockSpec((None, None, 1, d), lambda i: (l, i // tiles_per_seq, 0, 1)),
            pl.BlockSpec((None, d, IN_COLS_R), lambda i: (l, 0, 0)),
            pl.BlockSpec((None, 1, KV_LORA), lambda i: (l, 0, 0)),
            pl.BlockSpec((None, CONV_K, CONV_CH), lambda i: (l, 0, 0)),
            pl.BlockSpec((tm, ROPE_PAD), lambda i: (i % tiles_per_seq, 0)),
            pl.BlockSpec((tm, ROPE_PAD), lambda i: (i % tiles_per_seq, 0)),
        ],
        out_specs=(
            pl.BlockSpec((tm, Q_LORA), row),
            pl.BlockSpec((tm, KV_LORA), row),
            pl.BlockSpec((tm, QK_ROPE), row),
            pl.BlockSpec((tm, QK_W), row),
            pl.BlockSpec((tm, CONV_CH), row),
            pl.BlockSpec((None, CONV_K - 1, CONV_CH), lambda i: (i // tiles_per_seq, 0, 0)),
        ),
        scratch_shapes=[pltpu.VMEM((tm + 8, CONV_CH), F32)],
        compiler_params=_cparams(("arbitrary",)),
        name="premix_prompt",
    )(x, mod_p, mod_p, w_in_r, kv_g, conv_w, cos_t, sin_t)
    return outs


def _premix_sample(l, x, mod, w_in_r, kv_g, conv_w, cos_t, sin_t, e1, e2, wa, wb, dec_seq, tm):
    n, d = x.shape
    kern = functools.partial(_premix_sample_kernel, dec_seq=dec_seq)
    row = lambda i: (i, 0)
    nb = tm // dec_seq
    hq = N_HEADS * dec_seq
    outs = pl.pallas_call(
        kern,
        out_shape=(
            jax.ShapeDtypeStruct((n // dec_seq, hq, QK_W), F32),
            jax.ShapeDtypeStruct((n, KV_LORA), F32),
            jax.ShapeDtypeStruct((n, QK_ROPE), F32),
            jax.ShapeDtypeStruct((n, QK_W), F32),
            jax.ShapeDtypeStruct((n, CONV_CH), BF16),
            jax.ShapeDtypeStruct((n, CONV_CH), F32),
        ),
        grid=(n // tm,),
        in_specs=[
            pl.BlockSpec((tm, d), row),
            pl.BlockSpec((None, tm, d), lambda i: (l, i, 0)),
            pl.BlockSpec((None, tm, d), lambda i: (l, i, 1)),
            pl.BlockSpec((None, d, IN_COLS_R), lambda i: (l, 0, 0)),
            pl.BlockSpec((None, 1, KV_LORA), lambda i: (l, 0, 0)),
            pl.BlockSpec((None, CONV_K, CONV_CH), lambda i: (l, 0, 0)),
            pl.BlockSpec((tm, ROPE_PAD), row),
            pl.BlockSpec((tm, ROPE_PAD), row),
            pl.BlockSpec((tm, CONV_CH), row),
            pl.BlockSpec((tm, CONV_CH), row),
            pl.BlockSpec((None, N_HEADS, Q_LORA, QK_W), lambda i: (l, 0, 0, 0)),
            pl.BlockSpec((None, N_HEADS, Q_LORA, ROPE_PAD), lambda i: (l, 0, 0, 0)),
        ],
        out_specs=(
            pl.BlockSpec((nb, hq, QK_W), lambda i: (i, 0, 0)),
            pl.BlockSpec((tm, KV_LORA), row),
            pl.BlockSpec((tm, QK_ROPE), row),
            pl.BlockSpec((tm, QK_W), row),
            pl.BlockSpec((tm, CONV_CH), row),
            pl.BlockSpec((tm, CONV_CH), row),
        ),
        scratch_shapes=[pltpu.VMEM((tm + 8, CONV_CH), F32)],
        compiler_params=_cparams(("arbitrary",)),
        name="premix_sample",
    )(x, mod, mod, w_in_r, kv_g, conv_w, cos_t, sin_t, e1, e2, wa, wb)
    return outs


def _attn_prompt_kernel(qn_ref, k_ref, wa_ref, wb_ref, cos_ref, sin_ref, wuv_ref, o_ref,
                        q_s, m_s, l_s, acc_s, *, tq, tk):
    qi = pl.program_id(1)
    m_rows = N_HEADS * tq
    qn = qn_ref[...]
    for h in range(N_HEADS):
        qa = _dot(qn, wa_ref[h])
        qr = qa[:, KV_LORA:] * cos_ref[...] + _dot(qn, wb_ref[h]) * sin_ref[...]
        q_s[h * tq:(h + 1) * tq, :KV_LORA] = qa[:, :KV_LORA].astype(BF16)
        q_s[h * tq:(h + 1) * tq, KV_LORA:] = qr.astype(BF16)
    m_s[...] = jnp.full((m_rows, 1), NEG_BIG, F32)
    l_s[...] = jnp.zeros((m_rows, 1), F32)
    acc_s[...] = jnp.zeros((m_rows, KV_LORA), F32)

    def step(ki, masked):
        start = pl.multiple_of(ki * tk, tk)
        kt = k_ref[pl.ds(start, tk), :]
        s = _dot_nt(q_s[...], kt)
        if masked:
            q_pos = qi * tq + lax.broadcasted_iota(jnp.int32, (m_rows, tk), 0) % tq
            k_pos = start + lax.broadcasted_iota(jnp.int32, (m_rows, tk), 1)
            s = jnp.where(k_pos <= q_pos, s, NEG_BIG)
        m_old = m_s[...]
        m_new = jnp.maximum(m_old, jnp.max(s, axis=-1, keepdims=True))
        p = jnp.exp(s - m_new)
        alpha = jnp.exp(m_old - m_new)
        l_s[...] = alpha * l_s[...] + jnp.sum(p, axis=-1, keepdims=True)
        acc_s[...] = alpha * acc_s[...] + _dot(p.astype(BF16), kt[:, :KV_LORA])
        m_s[...] = m_new

    n_full = (qi * tq) // tk
    lax.fori_loop(0, n_full, lambda ki, c: (step(ki, False), c)[1], 0)
    n_diag = tq // tk if tq > tk else 1
    for j in range(n_diag):
        step(n_full + j, True)

    inv_l = 1.0 / l_s[...]
    heads = []
    for h in range(N_HEADS):
        o_h = (acc_s[h * tq:(h + 1) * tq, :] * inv_l[h * tq:(h + 1) * tq, :]).astype(BF16)
        heads.append(_dot(o_h, wuv_ref[h]))
    o_ref[...] = jnp.concatenate(heads, axis=1).astype(BF16)


def _attn_prompt(l, qn, kcat, wa, wb, cos_t, sin_t, wuv, batch, seq, tq, tk):
    n = qn.shape[0]
    nq = seq // tq
    kern = functools.partial(_attn_prompt_kernel, tq=tq, tk=tk)
    m_rows = N_HEADS * tq
    return pl.pallas_call(
        kern,
        out_shape=jax.ShapeDtypeStruct((n, N_HEADS * V_HEAD), BF16),
        grid=(batch, nq),
        in_specs=[
            pl.BlockSpec((tq, Q_LORA), lambda b, i: (b * nq + i, 0)),
            pl.BlockSpec((seq, QK_W), lambda b, i: (b, 0)),
            pl.BlockSpec((None, N_HEADS, Q_LORA, QK_W), lambda b, i: (l, 0, 0, 0)),
            pl.BlockSpec((None, N_HEADS, Q_LORA, ROPE_PAD), lambda b, i: (l, 0, 0, 0)),
            pl.BlockSpec((tq, ROPE_PAD), lambda b, i: (i, 0)),
            pl.BlockSpec((tq, ROPE_PAD), lambda b, i: (i, 0)),
            pl.BlockSpec((None, N_HEADS, KV_LORA, V_HEAD), lambda b, i: (l, 0, 0, 0)),
        ],
        out_specs=pl.BlockSpec((tq, N_HEADS * V_HEAD), lambda b, i: (b * nq + i, 0)),
        scratch_shapes=[
            pltpu.VMEM((m_rows, QK_W), BF16),
            pltpu.VMEM((m_rows, 1), F32),
            pltpu.VMEM((m_rows, 1), F32),
            pltpu.VMEM((m_rows, KV_LORA), F32),
        ],
        compiler_params=_cparams(("arbitrary", "arbitrary")),
        name="attn_prompt",
    )(qn, kcat, wa, wb, cos_t, sin_t, wuv)


def _attn_sample_kernel(pt_ref, qs_ref, knew_ref, lat_hbm, kpe_hbm, o_ref,
                        lat_buf, kpe_buf, s_buf, sems, *, layer, n_pages, dec_seq, chunk):
    b = pl.program_id(0)
    nb = pl.num_programs(0)
    hq = N_HEADS * dec_seq
    past = n_pages * PAGE_SIZE

    def page_copies(seq_idx, slot, j):
        page = pt_ref[seq_idx, j]
        dst = pl.ds(j * PAGE_SIZE, PAGE_SIZE)
        return (
            pltpu.make_async_copy(lat_hbm.at[layer, page], lat_buf.at[slot, dst, :],
                                  sems.at[slot, 0]),
            pltpu.make_async_copy(kpe_hbm.at[layer, page], kpe_buf.at[slot, dst, :],
                                  sems.at[slot, 1]),
        )

    def start_all(seq_idx, slot):
        def body(j, c):
            for cp in page_copies(seq_idx, slot, j):
                cp.start()
            return c
        lax.fori_loop(0, n_pages, body, 0)

    def wait_all(seq_idx, slot):
        def body(j, c):
            for cp in page_copies(seq_idx, slot, j):
                cp.wait()
            return c
        lax.fori_loop(0, n_pages, body, 0)

    slot = b % 2

    @pl.when(b == 0)
    def _():
        start_all(b, slot)

    @pl.when(b + 1 < nb)
    def _():
        start_all(b + 1, 1 - slot)

    wait_all(b, slot)

    q = qs_ref[...]
    q_lat = q[:, :KV_LORA].astype(BF16)
    q_pe = q[:, KV_LORA:KV_LORA + QK_ROPE].astype(BF16)
    n_chunks = past // chunk
    for c in range(n_chunks):
        rows = pl.ds(c * chunk, chunk)
        lat_c = lat_buf[slot, rows, :].astype(BF16)
        kpe_c = kpe_buf[slot, rows, :].astype(BF16)
        s_buf[:, c * chunk:(c + 1) * chunk] = _dot_nt(q_lat, lat_c) + _dot_nt(q_pe, kpe_c)

    knew = knew_ref[...]
    s_new = _dot_nt(q, knew)
    t_q = lax.broadcasted_iota(jnp.int32, (hq, dec_seq), 0) % dec_seq
    t_k = lax.broadcasted_iota(jnp.int32, (hq, dec_seq), 1)
    s_new = jnp.where(t_k <= t_q, s_new, NEG_BIG)

    s_past = s_buf[...]
    m = jnp.maximum(jnp.max(s_past, axis=-1, keepdims=True),
                    jnp.max(s_new, axis=-1, keepdims=True))
    p_new = jnp.exp(s_new - m)
    p_past = jnp.exp(s_past - m)
    denom = jnp.sum(p_past, axis=-1, keepdims=True) + jnp.sum(p_new, axis=-1, keepdims=True)
    s_buf[...] = p_past
    acc = _dot(p_new, knew[:, :KV_LORA])
    for c in range(n_chunks):
        rows = pl.ds(c * chunk, chunk)
        lat_c = lat_buf[slot, rows, :].astype(BF16)
        acc = acc + _dot(s_buf[:, c * chunk:(c + 1) * chunk].astype(BF16), lat_c)
    o_ref[...] = acc / denom


def _attn_sample(layer, page_table, qs, kcat_new, cache_latent, cache_krope, dec_seq):
    nb, hq, _ = qs.shape
    n_pages = page_table.shape[1]
    past = n_pages * PAGE_SIZE
    chunk = min(past, 1024)
    kern = functools.partial(_attn_sample_kernel, layer=layer, n_pages=n_pages,
                             dec_seq=dec_seq, chunk=chunk)
    grid_spec = pltpu.PrefetchScalarGridSpec(
        num_scalar_prefetch=1,
        grid=(nb,),
        in_specs=[
            pl.BlockSpec((None, hq, QK_W), lambda b, pt: (b, 0, 0)),
            pl.BlockSpec((dec_seq, QK_W), lambda b, pt: (b, 0)),
            pl.BlockSpec(memory_space=pl.ANY),
            pl.BlockSpec(memory_space=pl.ANY),
        ],
        out_specs=pl.BlockSpec((None, hq, KV_LORA), lambda b, pt: (b, 0, 0)),
        scratch_shapes=[
            pltpu.VMEM((2, past, KV_LORA), F32),
            pltpu.VMEM((2, past, QK_ROPE), F32),
            pltpu.VMEM((hq, past), F32),
            pltpu.SemaphoreType.DMA((2, 2)),
        ],
    )
    return pl.pallas_call(
        kern,
        out_shape=jax.ShapeDtypeStruct((nb, hq, KV_LORA), F32),
        grid_spec=grid_spec,
        compiler_params=_cparams(("arbitrary",)),
        name="attn_sample",
    )(page_table, qs, kcat_new, cache_latent, cache_krope)


def _uv_sample_kernel(ol_ref, wuv_ref, o_ref, *, dec_seq):
    nbk = ol_ref.shape[0]
    heads = []
    for h in range(N_HEADS):
        o_h = ol_ref[:, h * dec_seq:(h + 1) * dec_seq, :].reshape(nbk * dec_seq, KV_LORA)
        heads.append(_dot(o_h.astype(BF16), wuv_ref[h]))
    o_ref[...] = jnp.concatenate(heads, axis=1).astype(BF16)


def _uv_sample(l, o_lat, wuv, dec_seq, nbk):
    nb, hq, _ = o_lat.shape
    kern = functools.partial(_uv_sample_kernel, dec_seq=dec_seq)
    return pl.pallas_call(
        kern,
        out_shape=jax.ShapeDtypeStruct((nb * dec_seq, N_HEADS * V_HEAD), BF16),
        grid=(nb // nbk,),
        in_specs=[
            pl.BlockSpec((nbk, hq, KV_LORA), lambda i: (i, 0, 0)),
            pl.BlockSpec((None, N_HEADS, KV_LORA, V_HEAD), lambda i: (l, 0, 0, 0)),
        ],
        out_specs=pl.BlockSpec((nbk * dec_seq, N_HEADS * V_HEAD), lambda i: (i, 0)),
        compiler_params=_cparams(("arbitrary",)),
        name="uv_sample",
    )(o_lat, wuv)


def _postmix_kernel(x_ref, oa_ref, oc_ref, wo_ref, g1_ref, sh2_ref, sc2_ref, lg_ref, lb_ref,
                    x1_ref, h2_ref, *, alpha):
    half = oa_ref.shape[1]
    mix = _dot(oa_ref[...], wo_ref[:half, :]) + _dot(oc_ref[...], wo_ref[half:, :])
    x1 = _layer_norm(alpha * x_ref[...] + g1_ref[...] * mix, lg_ref[...], lb_ref[...])
    x1_ref[...] = x1
    h2_ref[...] = (x1 * (1.0 + sc2_ref[...]) + sh2_ref[...]).astype(BF16)


def _mod_spec(l, chunk, d, tm, tiles_per_seq):
    if tiles_per_seq is None:
        return pl.BlockSpec((None, tm, d), lambda i, *_: (l, i, chunk))
    return pl.BlockSpec((None, None, 1, d), lambda i, *_: (l, i // tiles_per_seq, 0, chunk))


def _postmix(l, x, o_att, o_conv, w_o, mod, ln_g, ln_b, alpha, tm, tiles_per_seq):
    n, d = x.shape
    half = o_att.shape[1]
    row = lambda i: (i, 0)
    vec = pl.BlockSpec((None, 1, d), lambda i: (l, 0, 0))
    return pl.pallas_call(
        functools.partial(_postmix_kernel, alpha=alpha),
        out_shape=(jax.ShapeDtypeStruct((n, d), F32), jax.ShapeDtypeStruct((n, d), BF16)),
        grid=(n // tm,),
        in_specs=[
            pl.BlockSpec((tm, d), row),
            pl.BlockSpec((tm, half), row),
            pl.BlockSpec((tm, half), row),
            pl.BlockSpec((None, 2 * half, d), lambda i: (l, 0, 0)),
            _mod_spec(l, 2, d, tm, tiles_per_seq),
            _mod_spec(l, 3, d, tm, tiles_per_seq),
            _mod_spec(l, 4, d, tm, tiles_per_seq),
            vec, vec,
        ],
        out_specs=(pl.BlockSpec((tm, d), row), pl.BlockSpec((tm, d), row)),
        compiler_params=_cparams(("arbitrary",)),
        name="postmix",
    )(x, o_att, o_conv, w_o, mod, mod, mod, ln_g, ln_b)


def _router_gates_t(h2, wrt_ref, rb_ref):
    tm = h2.shape[0]
    scores = jax.nn.sigmoid(_dot_nt(wrt_ref[...], h2))
    sel = scores + rb_ref[...]
    a = [sel[EXP_PER_GROUP * j:EXP_PER_GROUP * (j + 1), :] for j in range(EXP_PER_GROUP)]
    sc = [scores[EXP_PER_GROUP * j:EXP_PER_GROUP * (j + 1), :] for j in range(EXP_PER_GROUP)]

    def first_max(vals):
        mx = functools.reduce(jnp.maximum, vals)
        taken = jnp.zeros(vals[0].shape, jnp.bool_)
        firsts = []
        for v in vals:
            hit = jnp.logical_and(v == mx, jnp.logical_not(taken))
            firsts.append(hit)
            taken = jnp.logical_or(taken, hit)
        return mx, firsts

    m1, is1 = first_max(a)
    rest = [jnp.where(f, -jnp.inf, v) for f, v in zip(is1, a)]
    m2, is2 = first_max(rest)
    grp = m1 + m2
    rows = [grp[g:g + 1, :] for g in range(N_GROUPS)]
    gmax = functools.reduce(jnp.maximum, rows)
    gidx = jnp.full((1, tm), N_GROUPS - 1, jnp.int32)
    for g in range(N_GROUPS - 2, -1, -1):
        gidx = jnp.where(rows[g] == gmax, g, gidx)
    in_group = lax.broadcasted_iota(jnp.int32, (N_GROUPS, tm), 0) == gidx
    num = [jnp.where(jnp.logical_and(in_group, jnp.logical_or(f1, f2)), s, 0.0)
           for f1, f2, s in zip(is1, is2, sc)]
    tot = functools.reduce(jnp.add, num)
    denom = functools.reduce(jnp.add, [tot[g:g + 1, :] for g in range(N_GROUPS)])
    inv = 1.0 / denom
    return jnp.concatenate([v * inv for v in num], axis=0)


def _moe_dense_kernel(h2_ref, x1_ref, g2_ref, lg_ref, lb_ref, wrt_ref, rb_ref,
                      wg_ref, wu_ref, wd_ref, o_ref, gate_s, acc_s, *, alpha):
    e = pl.program_id(1)
    tm = h2_ref.shape[0]

    @pl.when(e == 0)
    def _():
        gt = _router_gates_t(h2_ref[...], wrt_ref, rb_ref)
        gt = jnp.concatenate([gt, jnp.zeros((LANES - N_EXPERTS, tm), F32)], axis=0)
        g = gt.T
        for ex in range(N_EXPERTS):
            r = EXP_PER_GROUP * (ex % EXP_PER_GROUP) + ex // EXP_PER_GROUP
            gate_s[ex] = jnp.broadcast_to(g[:, r:r + 1], (tm, LANES))
        acc_s[...] = jnp.zeros(acc_s.shape, F32)

    h2 = h2_ref[...]
    hid = jax.nn.silu(_dot(h2, wg_ref[...])) * _dot(h2, wu_ref[...])
    hid = hid * jnp.tile(gate_s[e], (1, hid.shape[1] // LANES))
    acc_s[...] += _dot(hid.astype(BF16), wd_ref[...])

    @pl.when(e == pl.num_programs(1) - 1)
    def _():
        y = alpha * x1_ref[...] + g2_ref[...] * acc_s[...]
        o_ref[...] = _layer_norm(y, lg_ref[...], lb_ref[...])


def _moe_dense(l, h2, x1, mod, ln_g, ln_b, wrt, rb, wg, wu, wd, alpha, tm, tiles_per_seq):
    n, d = x1.shape
    f = wg.shape[-1]
    row = lambda i, e: (i, 0)
    vec = pl.BlockSpec((None, 1, d), lambda i, e: (l, 0, 0))
    return pl.pallas_call(
        functools.partial(_moe_dense_kernel, alpha=alpha),
        out_shape=jax.ShapeDtypeStruct((n, d), F32),
        grid=(n // tm, N_EXPERTS),
        in_specs=[
            pl.BlockSpec((tm, d), row),
            pl.BlockSpec((tm, d), row),
            _mod_spec(l, 5, d, tm, tiles_per_seq),
            vec, vec,
            pl.BlockSpec((N_EXPERTS, d), lambda i, e: (0, 0)),
            pl.BlockSpec((N_EXPERTS, 1), lambda i, e: (0, 0)),
            pl.BlockSpec((None, None, d, f), lambda i, e: (l, e, 0, 0)),
            pl.BlockSpec((None, None, d, f), lambda i, e: (l, e, 0, 0)),
            pl.BlockSpec((None, None, f, d), lambda i, e: (l, e, 0, 0)),
        ],
        out_specs=pl.BlockSpec((tm, d), row),
        scratch_shapes=[
            pltpu.VMEM((N_EXPERTS, tm, LANES), F32),
            pltpu.VMEM((tm, d), F32),
        ],
        compiler_params=_cparams(("arbitrary", "arbitrary")),
        name="moe_dense",
    )(h2, x1, mod, ln_g, ln_b, wrt, rb, wg, wu, wd)


def _rope_tables(pos):
    half = QK_ROPE // 2
    inv = jnp.power(ROPE_THETA, -jnp.arange(half, dtype=F32) * 2.0 / QK_ROPE)
    ang = pos.astype(F32)[:, None] * inv[None, :]
    cos, sin = jnp.cos(ang), jnp.sin(ang)
    pad = jnp.zeros((pos.shape[0], ROPE_PAD - QK_ROPE), F32)
    return (jnp.concatenate([cos, cos, pad], axis=1),
            jnp.concatenate([-sin, sin, pad], axis=1))


def _swap_halves(w):
    half = QK_ROPE // 2
    return jnp.concatenate([w[..., half:], w[..., :half]], axis=-1)


def _pad_last(w, width):
    return jnp.pad(w, [(0, 0)] * (w.ndim - 1) + [(0, width - w.shape[-1])])


def kernel(x_prompt, x_sample, cache_latent, cache_krope, state_conv, page_table, c_prompt, c_sample,
           w_in, q_norm_g, w_uq, kv_norm_g, w_uk, w_uv, conv_w, w_o, ln1_g, ln1_b, ln2_g, ln2_b,
           w_ada, b_ada, w_router, router_bias, w_gate, w_up, w_down):
    batch, seq, d = x_prompt.shape
    dec_b, dec_seq, _ = x_sample.shape
    depth = w_in.shape[0]
    n_pages = page_table.shape[1]
    past_len = n_pages * PAGE_SIZE
    alpha = (2 * depth) ** 0.25
    n_p, n_s = batch * seq, dec_b * dec_seq

    s0, s1, s2, s3, s4 = (Q_LORA, Q_LORA + KV_LORA, Q_LORA + KV_LORA + QK_ROPE,
                          Q_LORA + KV_LORA + QK_ROPE + CONV_CH,
                          Q_LORA + KV_LORA + QK_ROPE + 2 * CONV_CH)
    w_kr = w_in[:, :, s1:s2]
    w_in_r = jnp.concatenate([
        w_in[:, :, :s1], w_in[:, :, s2:],
        _pad_last(w_kr, ROPE_PAD), _pad_last(_swap_halves(w_kr), ROPE_PAD)], axis=-1).astype(BF16)

    w_uq_t = jnp.transpose(w_uq, (0, 2, 1, 3))
    w_uk_t = jnp.transpose(w_uk, (0, 2, 1, 3))
    wuv = jnp.transpose(w_uv, (0, 2, 1, 3)).astype(BF16)
    qg = q_norm_g[:, :, None]
    w_lat = _fuse_q_weights(w_uq_t[..., :QK_NOPE], w_uk_t, qg)
    w_pe = w_uq_t[..., QK_NOPE:] * qg[:, None] * SM_SCALE
    wa = jnp.concatenate([w_lat, _pad_last(w_pe, ROPE_PAD)], axis=-1).astype(BF16)
    wb = _pad_last(_swap_halves(w_pe), ROPE_PAD).astype(BF16)

    w_o_b = w_o.astype(BF16)
    wg_b, wu_b, wd_b = w_gate.astype(BF16), w_up.astype(BF16), w_down.astype(BF16)
    perm = jnp.arange(N_EXPERTS).reshape(N_GROUPS, EXP_PER_GROUP).T.reshape(-1)
    wrt = w_router.T[perm].astype(BF16)
    rb = router_bias[perm].reshape(N_EXPERTS, 1)

    kv_g = kv_norm_g.reshape(depth, 1, KV_LORA)
    ln1g, ln1b = ln1_g.reshape(depth, 1, d), ln1_b.reshape(depth, 1, d)
    ln2g, ln2b = ln2_g.reshape(depth, 1, d), ln2_b.reshape(depth, 1, d)

    cos_p, sin_p = _rope_tables(jnp.arange(seq))
    cos_s, sin_s = _rope_tables(past_len + jnp.arange(n_s) % dec_seq)

    c_all = jnp.concatenate([jnp.repeat(c_sample, dec_seq, axis=0), c_prompt], axis=0)
    mod = _ada_mod(c_all, w_ada, b_ada)
    mod_p = mod[:, n_s:].reshape(depth, batch, 1, 6 * d)

    tm_p = min(512, seq)
    tps = seq // tm_p
    tq = min(256, seq)
    tk = tq
    tm_s = min(256, n_s)

    xp = x_prompt.reshape(n_p, d)
    xs = x_sample.reshape(n_s, d)
    outs = [[] for _ in range(6)]
    for l in range(depth):
        qn, ckv, kpe, kcat, oconv, cst = _premix_prompt(
            l, xp, mod_p, w_in_r, kv_g, conv_w, cos_p, sin_p, batch, seq, tm_p)
        o_att = _attn_prompt(l, qn, kcat, wa, wb, cos_p, sin_p, wuv, batch, seq, tq, tk)
        x1, h2 = _postmix(l, xp, o_att, oconv, w_o_b, mod_p, ln1g, ln1b, alpha, tm_p, tps)
        xp = _moe_dense(l, h2, x1, mod_p, ln2g, ln2b, wrt, rb, wg_b, wu_b, wd_b, alpha, tm_p, tps)
        outs[0].append(ckv.reshape(batch, seq, KV_LORA))
        outs[1].append(kpe.reshape(batch, seq, QK_ROPE))
        outs[2].append(cst)
        st = state_conv[l]
        pad_n = dec_seq - 1
        e1 = jnp.pad(st[:, 1:2], ((0, 0), (0, pad_n), (0, 0))).reshape(n_s, CONV_CH)
        e2 = jnp.pad(st, ((0, 0), (0, dec_seq - 2), (0, 0))).reshape(n_s, CONV_CH)
        qs, ckv_s, kpe_s, kcat_s, oconv_s, u_s = _premix_sample(
            l, xs, mod, w_in_r, kv_g, conv_w, cos_s, sin_s, e1, e2, wa, wb, dec_seq, tm_s)
        o_lat = _attn_sample(l, page_table, qs, kcat_s, cache_latent, cache_krope, dec_seq)
        o_att_s = _uv_sample(l, o_lat, wuv, dec_seq, tm_s // dec_seq)
        x1s, h2s = _postmix(l, xs, o_att_s, oconv_s, w_o_b, mod, ln1g, ln1b, alpha, tm_s, None)
        xs = _moe_dense(l, h2s, x1s, mod, ln2g, ln2b, wrt, rb, wg_b, wu_b, wd_b, alpha, tm_s, None)
        outs[3].append(ckv_s.reshape(dec_b, dec_seq, KV_LORA))
        outs[4].append(kpe_s.reshape(dec_b, dec_seq, QK_ROPE))
        outs[5].append(u_s.reshape(dec_b, dec_seq, CONV_CH)[:, dec_seq - (CONV_K - 1):])
    return (xp.reshape(batch, seq, d), xs.reshape(dec_b, dec_seq, d),
            jnp.stack(outs[0]), jnp.stack(outs[1]), jnp.stack(outs[2]),
            jnp.stack(outs[3]), jnp.stack(outs[4]), jnp.stack(outs[5]))
```

```python
import functools

import jax
import jax.numpy as jnp
from jax import lax
from jax.experimental import pallas as pl
from jax.experimental.pallas import tpu as pltpu

N_HEADS = 8
QK_NOPE = 64
QK_ROPE = 32
V_HEAD = 64
Q_LORA = 256
KV_LORA = 256
ROPE_THETA = 10000.0
SM_SCALE = (QK_NOPE + QK_ROPE) ** -0.5
LOG2_E = 1.4426950408889634
Q_SCALE = SM_SCALE * LOG2_E
PAGE_SIZE = 128
CONV_K = 3
N_EXPERTS = 16
N_GROUPS = 4
EXP_PER_GROUP = N_EXPERTS // N_GROUPS
LN_EPS = 1e-5
RMS_EPS = 1e-6

LANES = 128
ROPE_PAD = LANES
QK_W = KV_LORA + ROPE_PAD
NEG_BIG = -1e30
VMEM_LIMIT = 56 * 1024 * 1024

BF16 = jnp.bfloat16
F32 = jnp.float32


def _cparams(sem):
    return pltpu.CompilerParams(dimension_semantics=sem, vmem_limit_bytes=VMEM_LIMIT)


def _dot(a, b):
    return jnp.dot(a, b, preferred_element_type=F32)


def _dot_nt(a, b):
    return lax.dot_general(a, b, (((1,), (1,)), ((), ())), preferred_element_type=F32)


def _layer_norm(y, g, b):
    mu = jnp.mean(y, axis=-1, keepdims=True)
    d = y - mu
    var = jnp.mean(d * d, axis=-1, keepdims=True)
    return d * lax.rsqrt(var + LN_EPS) * g + b


def _rms_scale(v):
    return lax.rsqrt(jnp.mean(v * v, axis=-1, keepdims=True) + RMS_EPS)


def _ada_kernel(c_ref, w_ref, b_ref, o_ref):
    o_ref[...] = _dot(c_ref[...], w_ref[...].astype(BF16)) + b_ref[...]


def _ada_mod(c_all, w_ada, b_ada):
    depth, d, d6 = w_ada.shape
    n = c_all.shape[0]
    tn = 512
    return pl.pallas_call(
        _ada_kernel,
        out_shape=jax.ShapeDtypeStruct((depth, n, d6), F32),
        grid=(depth, d6 // tn),
        in_specs=[
            pl.BlockSpec((n, d), lambda l, j: (0, 0)),
            pl.BlockSpec((None, d, tn), lambda l, j: (l, 0, j)),
            pl.BlockSpec((None, 1, tn), lambda l, j: (l, 0, j)),
        ],
        out_specs=pl.BlockSpec((None, n, tn), lambda l, j: (l, 0, j)),
        compiler_params=_cparams(("arbitrary", "arbitrary")),
        name="ada_mod",
    )(c_all.astype(BF16), w_ada, b_ada.reshape(depth, 1, d6))


def _qfuse_kernel(uq_ref, uk_ref, g_ref, o_ref):
    a = uq_ref[...] * g_ref[...] * Q_SCALE
    o_ref[...] = lax.dot_general(a, uk_ref[...], (((1,), (1,)), ((), ())),
                                 preferred_element_type=F32,
                                 precision=lax.Precision.HIGHEST)


def _fuse_q_weights(w_uq_nope_t, w_uk_t, q_norm_g):
    depth = w_uq_nope_t.shape[0]
    return pl.pallas_call(
        _qfuse_kernel,
        out_shape=jax.ShapeDtypeStruct((depth, N_HEADS, Q_LORA, KV_LORA), F32),
        grid=(depth, N_HEADS),
        in_specs=[
            pl.BlockSpec((None, None, Q_LORA, QK_NOPE), lambda l, h: (l, h, 0, 0)),
            pl.BlockSpec((None, None, KV_LORA, QK_NOPE), lambda l, h: (l, h, 0, 0)),
            pl.BlockSpec((None, Q_LORA, 1), lambda l, h: (l, 0, 0)),
        ],
        out_specs=pl.BlockSpec((None, None, Q_LORA, KV_LORA), lambda l, h: (l, h, 0, 0)),
        compiler_params=_cparams(("arbitrary", "arbitrary")),
        name="fuse_q_weights",
    )(w_uq_nope_t, w_uk_t, q_norm_g)


C_QA = 0
C_KV = C_QA + Q_LORA
C_GB = C_KV + KV_LORA
CONV_CH = 512
C_GC = C_GB + CONV_CH
C_V = C_GC + CONV_CH
C_KR = C_V + CONV_CH
C_KRS = C_KR + ROPE_PAD
IN_COLS_R = C_KRS + ROPE_PAD


def _premix_common(x_ref, sh_ref, sc_ref, w_ref, kvg_ref, cw_ref, cos_ref, sin_ref):
    h = (x_ref[...] * (1.0 + sc_ref[...]) + sh_ref[...]).astype(BF16)
    q_a = _dot(h, w_ref[:, C_QA:C_KV])
    qn = (q_a * _rms_scale(q_a)).astype(BF16)
    kv_a = _dot(h, w_ref[:, C_KV:C_GB])
    ckv = kv_a * _rms_scale(kv_a) * kvg_ref[...]
    kr = _dot(h, w_ref[:, C_KR:C_KRS])
    krs = _dot(h, w_ref[:, C_KRS:IN_COLS_R])
    kpe = kr * cos_ref[...] + krs * sin_ref[...]
    gate_b = _dot(h, w_ref[:, C_GB:C_GC])
    u = _dot(h, w_ref[:, C_GC:C_V]) * _dot(h, w_ref[:, C_V:C_KR])
    return qn, ckv, kpe, gate_b, u


def _premix_prompt_kernel(x_ref, sh_ref, sc_ref, w_ref, kvg_ref, cw_ref, cos_ref, sin_ref,
                          qn_ref, ckv_ref, kpe_ref, kcat_ref, latt_ref, oconv_ref, cst_ref,
                          ubuf, *, tiles_per_seq):
    tm = x_ref.shape[0]
    qn, ckv, kpe, gate_b, u = _premix_common(x_ref, sh_ref, sc_ref, w_ref, kvg_ref, cw_ref,
                                             cos_ref, sin_ref)
    qn_ref[...] = qn
    ckv_ref[...] = ckv
    kpe_ref[...] = kpe[:, :QK_ROPE]
    kcat_ref[:, :KV_LORA] = ckv.astype(BF16)
    kcat_ref[:, KV_LORA:] = kpe.astype(BF16)
    latt_ref[...] = ckv.T.astype(BF16)

    @pl.when(pl.program_id(0) % tiles_per_seq == 0)
    def _():
        ubuf[0:8, :] = jnp.zeros((8, CONV_CH), F32)

    ubuf[8:8 + tm, :] = u
    y = (ubuf[6:6 + tm, :] * cw_ref[0:1, :] + ubuf[7:7 + tm, :] * cw_ref[1:2, :]
         + u * cw_ref[2:3, :])
    oconv_ref[...] = (gate_b * y).astype(BF16)
    cst_ref[...] = u[tm - (CONV_K - 1):, :]
    ubuf[0:8, :] = u[tm - 8:, :]


def _premix_sample_kernel(x_ref, sh_ref, sc_ref, w_ref, kvg_ref, cw_ref, cos_ref, sin_ref,
                          e1_ref, e2_ref, wa_ref, wb_ref,
                          qs_ref, ckv_ref, kpe_ref, kcat_ref, oconv_ref, u_ref,
                          ubuf, *, dec_seq):
    tm = x_ref.shape[0]
    qn, ckv, kpe, gate_b, u = _premix_common(x_ref, sh_ref, sc_ref, w_ref, kvg_ref, cw_ref,
                                             cos_ref, sin_ref)
    ckv_ref[...] = ckv
    kpe_ref[...] = kpe[:, :QK_ROPE]
    kcat_ref[:, :KV_LORA] = ckv
    kcat_ref[:, KV_LORA:] = kpe
    u_ref[...] = u

    ubuf[0:8, :] = jnp.zeros((8, CONV_CH), F32)
    ubuf[8:8 + tm, :] = u
    t = lax.broadcasted_iota(jnp.int32, (tm, CONV_CH), 0) % dec_seq
    u1 = jnp.where(t >= 1, ubuf[7:7 + tm, :], e1_ref[...])
    u2 = jnp.where(t >= 2, ubuf[6:6 + tm, :], e2_ref[...])
    y = u2 * cw_ref[0:1, :] + u1 * cw_ref[1:2, :] + u * cw_ref[2:3, :]
    oconv_ref[...] = (gate_b * y).astype(BF16)

    nb = tm // dec_seq
    for h in range(N_HEADS):
        qa = _dot(qn, wa_ref[h])
        qr = qa[:, KV_LORA:] * cos_ref[...] + _dot(qn, wb_ref[h]) * sin_ref[...]
        qs_ref[:, h * dec_seq:(h + 1) * dec_seq, :KV_LORA] = (
            qa[:, :KV_LORA].reshape(nb, dec_seq, KV_LORA))
        qs_ref[:, h * dec_seq:(h + 1) * dec_seq, KV_LORA:] = qr.reshape(nb, dec_seq, ROPE_PAD)


def _premix_prompt(l, x, mod_p, w_in_r, kv_g, conv_w, cos_t, sin_t, batch, seq, tm):
    n, d = x.shape
    tiles_per_seq = seq // tm
    kern = functools.partial(_premix_prompt_kernel, tiles_per_seq=tiles_per_seq)
    row = lambda i: (i, 0)
    outs = pl.pallas_call(
        kern,
        out_shape=(
            jax.ShapeDtypeStruct((n, Q_LORA), BF16),
            jax.ShapeDtypeStruct((n, KV_LORA), F32),
            jax.ShapeDtypeStruct((n, QK_ROPE), F32),
            jax.ShapeDtypeStruct((n, QK_W), BF16),
            jax.ShapeDtypeStruct((KV_LORA, n), BF16),
            jax.ShapeDtypeStruct((n, CONV_CH), BF16),
            jax.ShapeDtypeStruct((batch, CONV_K - 1, CONV_CH), F32),
        ),
        grid=(n // tm,),
        in_specs=[
            pl.BlockSpec((tm, d), row),
            pl.BlockSpec((None, None, 1, d), lambda i: (l, i // tiles_per_seq, 0, 0)),
            pl.BlockSpec((None, None, 1, d), lambda i: (l, i // tiles_per_seq, 0, 1)),
            pl.BlockSpec((None, d, IN_COLS_R), lambda i: (l, 0, 0)),
            pl.BlockSpec((None, 1, KV_LORA), lambda i: (l, 0, 0)),
            pl.BlockSpec((None, CONV_K, CONV_CH), lambda i: (l, 0, 0)),
            pl.BlockSpec((tm, ROPE_PAD), lambda i: (i % tiles_per_seq, 0)),
            pl.BlockSpec((tm, ROPE_PAD), lambda i: (i % tiles_per_seq, 0)),
        ],
        out_specs=(
            pl.BlockSpec((tm, Q_LORA), row),
            pl.BlockSpec((tm, KV_LORA), row),
            pl.BlockSpec((tm, QK_ROPE), row),
            pl.BlockSpec((tm, QK_W), row),
            pl.BlockSpec((KV_LORA, tm), lambda i: (0, i)),
            pl.BlockSpec((tm, CONV_CH), row),
            pl.BlockSpec((None, CONV_K - 1, CONV_CH), lambda i: (i // tiles_per_seq, 0, 0)),
        ),
        scratch_shapes=[pltpu.VMEM((tm + 8, CONV_CH), F32)],
        compiler_params=_cparams(("arbitrary",)),
        name="premix_prompt",
    )(x, mod_p, mod_p, w_in_r, kv_g, conv_w, cos_t, sin_t)
    return outs


def _premix_sample(l, x, mod, w_in_r, kv_g, conv_w, cos_t, sin_t, e1, e2, wa, wb, dec_seq, tm):
    n, d = x.shape
    kern = functools.partial(_premix_sample_kernel, dec_seq=dec_seq)
    row = lambda i: (i, 0)
    nb = tm // dec_seq
    hq = N_HEADS * dec_seq
    outs = pl.pallas_call(
        kern,
        out_shape=(
            jax.ShapeDtypeStruct((n // dec_seq, hq, QK_W), F32),
            jax.ShapeDtypeStruct((n, KV_LORA), F32),
            jax.ShapeDtypeStruct((n, QK_ROPE), F32),
            jax.ShapeDtypeStruct((n, QK_W), F32),
            jax.ShapeDtypeStruct((n, CONV_CH), BF16),
            jax.ShapeDtypeStruct((n, CONV_CH), F32),
        ),
        grid=(n // tm,),
        in_specs=[
            pl.BlockSpec((tm, d), row),
            pl.BlockSpec((None, tm, d), lambda i: (l, i, 0)),
            pl.BlockSpec((None, tm, d), lambda i: (l, i, 1)),
            pl.BlockSpec((None, d, IN_COLS_R), lambda i: (l, 0, 0)),
            pl.BlockSpec((None, 1, KV_LORA), lambda i: (l, 0, 0)),
            pl.BlockSpec((None, CONV_K, CONV_CH), lambda i: (l, 0, 0)),
            pl.BlockSpec((tm, ROPE_PAD), row),
            pl.BlockSpec((tm, ROPE_PAD), row),
            pl.BlockSpec((tm, CONV_CH), row),
            pl.BlockSpec((tm, CONV_CH), row),
            pl.BlockSpec((None, N_HEADS, Q_LORA, QK_W), lambda i: (l, 0, 0, 0)),
            pl.BlockSpec((None, N_HEADS, Q_LORA, ROPE_PAD), lambda i: (l, 0, 0, 0)),
        ],
        out_specs=(
            pl.BlockSpec((nb, hq, QK_W), lambda i: (i, 0, 0)),
            pl.BlockSpec((tm, KV_LORA), row),
            pl.BlockSpec((tm, QK_ROPE), row),
            pl.BlockSpec((tm, QK_W), row),
            pl.BlockSpec((tm, CONV_CH), row),
            pl.BlockSpec((tm, CONV_CH), row),
        ),
        scratch_shapes=[pltpu.VMEM((tm + 8, CONV_CH), F32)],
        compiler_params=_cparams(("arbitrary",)),
        name="premix_sample",
    )(x, mod, mod, w_in_r, kv_g, conv_w, cos_t, sin_t, e1, e2, wa, wb)
    return outs


def _attn_prompt_kernel(qn_ref, k_ref, vt_ref, wa_ref, wb_ref, cos_ref, sin_ref, wuvt_ref, o_ref,
                        q_s, m_s, l_s, acc_s, *, tq, tk):
    qi = pl.program_id(1)
    qn = qn_ref[...]
    for h in range(N_HEADS):
        qa = _dot(qn, wa_ref[h])
        qr = qa[:, KV_LORA:] * cos_ref[...] + _dot(qn, wb_ref[h]) * sin_ref[...]
        q_s[h, :, :KV_LORA] = qa[:, :KV_LORA].astype(BF16)
        q_s[h, :, KV_LORA:] = qr.astype(BF16)
    m_s[...] = jnp.full(m_s.shape, NEG_BIG, F32)
    l_s[...] = jnp.zeros(l_s.shape, F32)
    acc_s[...] = jnp.zeros(acc_s.shape, F32)

    def step(ki, masked):
        start = pl.multiple_of(ki * tk, tk)
        kt = k_ref[pl.ds(start, tk), :]
        vt = vt_ref[:, pl.ds(start, tk)]
        if masked:
            k_pos = start + lax.broadcasted_iota(jnp.int32, (tk, tq), 0)
            q_pos = qi * tq + lax.broadcasted_iota(jnp.int32, (tk, tq), 1)
            keep = k_pos <= q_pos
        for h in range(N_HEADS):
            s = _dot_nt(kt, q_s[h])
            if masked:
                s = jnp.where(keep, s, NEG_BIG)
            m_old = m_s[h]
            m_new = jnp.maximum(m_old, jnp.max(s, axis=0, keepdims=True))
            p = jnp.exp2(s - m_new)
            alpha = jnp.exp2(m_old - m_new)
            l_s[h] = alpha * l_s[h] + jnp.sum(p, axis=0, keepdims=True)
            acc_s[h] = alpha * acc_s[h] + _dot(vt, p.astype(BF16))
            m_s[h] = m_new

    n_full = (qi * tq) // tk
    lax.fori_loop(0, n_full, lambda ki, c: (step(ki, False), c)[1], 0)
    n_diag = tq // tk if tq > tk else 1
    for j in range(n_diag):
        step(n_full + j, True)

    heads = []
    for h in range(N_HEADS):
        o_t = (acc_s[h] * (1.0 / l_s[h])).astype(BF16)
        heads.append(_dot(wuvt_ref[h], o_t))
    o_ref[...] = jnp.concatenate(heads, axis=0).T.astype(BF16)


def _attn_prompt(l, qn, kcat, latt, wa, wb, cos_t, sin_t, wuvt, batch, seq, tq, tk):
    n = qn.shape[0]
    nq = seq // tq
    kern = functools.partial(_attn_prompt_kernel, tq=tq, tk=tk)
    return pl.pallas_call(
        kern,
        out_shape=jax.ShapeDtypeStruct((n, N_HEADS * V_HEAD), BF16),
        grid=(batch, nq),
        in_specs=[
            pl.BlockSpec((tq, Q_LORA), lambda b, i: (b * nq + i, 0)),
            pl.BlockSpec((seq, QK_W), lambda b, i: (b, 0)),
            pl.BlockSpec((KV_LORA, seq), lambda b, i: (0, b)),
            pl.BlockSpec((None, N_HEADS, Q_LORA, QK_W), lambda b, i: (l, 0, 0, 0)),
            pl.BlockSpec((None, N_HEADS, Q_LORA, ROPE_PAD), lambda b, i: (l, 0, 0, 0)),
            pl.BlockSpec((tq, ROPE_PAD), lambda b, i: (i, 0)),
            pl.BlockSpec((tq, ROPE_PAD), lambda b, i: (i, 0)),
            pl.BlockSpec((None, N_HEADS, V_HEAD, KV_LORA), lambda b, i: (l, 0, 0, 0)),
        ],
        out_specs=pl.BlockSpec((tq, N_HEADS * V_HEAD), lambda b, i: (b * nq + i, 0)),
        scratch_shapes=[
            pltpu.VMEM((N_HEADS, tq, QK_W), BF16),
            pltpu.VMEM((N_HEADS, 1, tq), F32),
            pltpu.VMEM((N_HEADS, 1, tq), F32),
            pltpu.VMEM((N_HEADS, KV_LORA, tq), F32),
        ],
        compiler_params=_cparams(("arbitrary", "arbitrary")),
        name="attn_prompt",
    )(qn, kcat, latt, wa, wb, cos_t, sin_t, wuvt)


def _attn_sample_kernel(pt_ref, qs_ref, knew_ref, lat_hbm, kpe_hbm, o_ref,
                        lat_buf, kpe_buf, s_buf, sems, *, layer, n_pages, dec_seq, chunk):
    b = pl.program_id(0)
    nb = pl.num_programs(0)
    hq = N_HEADS * dec_seq
    past = n_pages * PAGE_SIZE

    def page_copies(seq_idx, slot, j):
        page = pt_ref[seq_idx, j]
        dst = pl.ds(j * PAGE_SIZE, PAGE_SIZE)
        return (
            pltpu.make_async_copy(lat_hbm.at[layer, page], lat_buf.at[slot, dst, :],
                                  sems.at[slot, 0]),
            pltpu.make_async_copy(kpe_hbm.at[layer, page], kpe_buf.at[slot, j],
                                  sems.at[slot, 1]),
        )

    def start_all(seq_idx, slot):
        def body(j, c):
            for cp in page_copies(seq_idx, slot, j):
                cp.start()
            return c
        lax.fori_loop(0, n_pages, body, 0)

    def wait_all(seq_idx, slot):
        def body(j, c):
            for cp in page_copies(seq_idx, slot, j):
                cp.wait()
            return c
        lax.fori_loop(0, n_pages, body, 0)

    slot = b % 2

    @pl.when(b == 0)
    def _():
        start_all(b, slot)

    @pl.when(b + 1 < nb)
    def _():
        start_all(b + 1, 1 - slot)

    wait_all(b, slot)

    q = qs_ref[...]
    q_lat = q[:, :KV_LORA].astype(BF16)
    q_pe = q[:, KV_LORA:KV_LORA + QK_ROPE].astype(BF16)
    n_chunks = past // chunk
    pages_per_chunk = chunk // PAGE_SIZE
    for c in range(n_chunks):
        rows = pl.ds(c * chunk, chunk)
        lat_c = lat_buf[slot, rows, :].astype(BF16)
        s_pe = jnp.concatenate(
            [_dot(q_pe, kpe_buf[slot, c * pages_per_chunk + j].astype(BF16))
             for j in range(pages_per_chunk)], axis=1)
        s_buf[:, c * chunk:(c + 1) * chunk] = _dot_nt(q_lat, lat_c) + s_pe

    knew = knew_ref[...]
    s_new = _dot_nt(q, knew)
    t_q = lax.broadcasted_iota(jnp.int32, (hq, dec_seq), 0) % dec_seq
    t_k = lax.broadcasted_iota(jnp.int32, (hq, dec_seq), 1)
    s_new = jnp.where(t_k <= t_q, s_new, NEG_BIG)

    s_past = s_buf[...]
    m = jnp.maximum(jnp.max(s_past, axis=-1, keepdims=True),
                    jnp.max(s_new, axis=-1, keepdims=True))
    p_new = jnp.exp2(s_new - m)
    p_past = jnp.exp2(s_past - m)
    denom = jnp.sum(p_past, axis=-1, keepdims=True) + jnp.sum(p_new, axis=-1, keepdims=True)
    s_buf[...] = p_past
    acc = _dot(p_new, knew[:, :KV_LORA])
    for c in range(n_chunks):
        rows = pl.ds(c * chunk, chunk)
        lat_c = lat_buf[slot, rows, :].astype(BF16)
        acc = acc + _dot(s_buf[:, c * chunk:(c + 1) * chunk].astype(BF16), lat_c)
    o_ref[...] = acc / denom


def _attn_sample(layer, page_table, qs, kcat_new, cache_latent, cache_krope, dec_seq):
    nb, hq, _ = qs.shape
    n_pages = page_table.shape[1]
    past = n_pages * PAGE_SIZE
    chunk = min(past, 1024)
    kern = functools.partial(_attn_sample_kernel, layer=layer, n_pages=n_pages,
                             dec_seq=dec_seq, chunk=chunk)
    grid_spec = pltpu.PrefetchScalarGridSpec(
        num_scalar_prefetch=1,
        grid=(nb,),
        in_specs=[
            pl.BlockSpec((None, hq, QK_W), lambda b, pt: (b, 0, 0)),
            pl.BlockSpec((dec_seq, QK_W), lambda b, pt: (b, 0)),
            pl.BlockSpec(memory_space=pl.ANY),
            pl.BlockSpec(memory_space=pl.ANY),
        ],
        out_specs=pl.BlockSpec((None, hq, KV_LORA), lambda b, pt: (b, 0, 0)),
        scratch_shapes=[
            pltpu.VMEM((2, past, KV_LORA), F32),
            pltpu.VMEM((2, n_pages, QK_ROPE, PAGE_SIZE), F32),
            pltpu.VMEM((hq, past), F32),
            pltpu.SemaphoreType.DMA((2, 2)),
        ],
    )
    return pl.pallas_call(
        kern,
        out_shape=jax.ShapeDtypeStruct((nb, hq, KV_LORA), F32),
        grid_spec=grid_spec,
        compiler_params=_cparams(("arbitrary",)),
        name="attn_sample",
    )(page_table, qs, kcat_new, cache_latent, cache_krope)


def _uv_sample_kernel(ol_ref, wuv_ref, o_ref, *, dec_seq):
    nbk = ol_ref.shape[0]
    heads = []
    for h in range(N_HEADS):
        o_h = ol_ref[:, h * dec_seq:(h + 1) * dec_seq, :].reshape(nbk * dec_seq, KV_LORA)
        heads.append(_dot(o_h.astype(BF16), wuv_ref[h]))
    o_ref[...] = jnp.concatenate(heads, axis=1).astype(BF16)


def _uv_sample(l, o_lat, wuv, dec_seq, nbk):
    nb, hq, _ = o_lat.shape
    kern = functools.partial(_uv_sample_kernel, dec_seq=dec_seq)
    return pl.pallas_call(
        kern,
        out_shape=jax.ShapeDtypeStruct((nb * dec_seq, N_HEADS * V_HEAD), BF16),
        grid=(nb // nbk,),
        in_specs=[
            pl.BlockSpec((nbk, hq, KV_LORA), lambda i: (i, 0, 0)),
            pl.BlockSpec((None, N_HEADS, KV_LORA, V_HEAD), lambda i: (l, 0, 0, 0)),
        ],
        out_specs=pl.BlockSpec((nbk * dec_seq, N_HEADS * V_HEAD), lambda i: (i, 0)),
        compiler_params=_cparams(("arbitrary",)),
        name="uv_sample",
    )(o_lat, wuv)


def _postmix_kernel(x_ref, oa_ref, oc_ref, wo_ref, g1_ref, sh2_ref, sc2_ref, lg_ref, lb_ref,
                    x1_ref, h2_ref, *, alpha):
    half = oa_ref.shape[1]
    mix = _dot(oa_ref[...], wo_ref[:half, :]) + _dot(oc_ref[...], wo_ref[half:, :])
    x1 = _layer_norm(alpha * x_ref[...] + g1_ref[...] * mix, lg_ref[...], lb_ref[...])
    x1_ref[...] = x1
    h2_ref[...] = (x1 * (1.0 + sc2_ref[...]) + sh2_ref[...]).astype(BF16)


def _mod_spec(l, chunk, d, tm, tiles_per_seq):
    if tiles_per_seq is None:
        return pl.BlockSpec((None, tm, d), lambda i, *_: (l, i, chunk))
    return pl.BlockSpec((None, None, 1, d), lambda i, *_: (l, i // tiles_per_seq, 0, chunk))


def _postmix(l, x, o_att, o_conv, w_o, mod, ln_g, ln_b, alpha, tm, tiles_per_seq):
    n, d = x.shape
    half = o_att.shape[1]
    row = lambda i: (i, 0)
    vec = pl.BlockSpec((None, 1, d), lambda i: (l, 0, 0))
    return pl.pallas_call(
        functools.partial(_postmix_kernel, alpha=alpha),
        out_shape=(jax.ShapeDtypeStruct((n, d), F32), jax.ShapeDtypeStruct((n, d), BF16)),
        grid=(n // tm,),
        in_specs=[
            pl.BlockSpec((tm, d), row),
            pl.BlockSpec((tm, half), row),
            pl.BlockSpec((tm, half), row),
            pl.BlockSpec((None, 2 * half, d), lambda i: (l, 0, 0)),
            _mod_spec(l, 2, d, tm, tiles_per_seq),
            _mod_spec(l, 3, d, tm, tiles_per_seq),
            _mod_spec(l, 4, d, tm, tiles_per_seq),
            vec, vec,
        ],
        out_specs=(pl.BlockSpec((tm, d), row), pl.BlockSpec((tm, d), row)),
        compiler_params=_cparams(("arbitrary",)),
        name="postmix",
    )(x, o_att, o_conv, w_o, mod, mod, mod, ln_g, ln_b)


def _router_gates_t(h2, wrt_ref, rb_ref):
    tm = h2.shape[0]
    scores = jax.nn.sigmoid(_dot_nt(wrt_ref[...], h2))
    sel = scores + rb_ref[...]
    a = [sel[EXP_PER_GROUP * j:EXP_PER_GROUP * (j + 1), :] for j in range(EXP_PER_GROUP)]
    sc = [scores[EXP_PER_GROUP * j:EXP_PER_GROUP * (j + 1), :] for j in range(EXP_PER_GROUP)]

    def first_max(vals):
        mx = functools.reduce(jnp.maximum, vals)
        taken = jnp.zeros(vals[0].shape, jnp.bool_)
        firsts = []
        for v in vals:
            hit = jnp.logical_and(v == mx, jnp.logical_not(taken))
            firsts.append(hit)
            taken = jnp.logical_or(taken, hit)
        return mx, firsts

    m1, is1 = first_max(a)
    rest = [jnp.where(f, -jnp.inf, v) for f, v in zip(is1, a)]
    m2, is2 = first_max(rest)
    grp = m1 + m2
    rows = [grp[g:g + 1, :] for g in range(N_GROUPS)]
    gmax = functools.reduce(jnp.maximum, rows)
    gidx = jnp.full((1, tm), N_GROUPS - 1, jnp.int32)
    for g in range(N_GROUPS - 2, -1, -1):
        gidx = jnp.where(rows[g] == gmax, g, gidx)
    in_group = lax.broadcasted_iota(jnp.int32, (N_GROUPS, tm), 0) == gidx
    num = [jnp.where(jnp.logical_and(in_group, jnp.logical_or(f1, f2)), s, 0.0)
           for f1, f2, s in zip(is1, is2, sc)]
    tot = functools.reduce(jnp.add, num)
    denom = functools.reduce(jnp.add, [tot[g:g + 1, :] for g in range(N_GROUPS)])
    inv = 1.0 / denom
    return jnp.concatenate([v * inv for v in num], axis=0)


def _moe_dense_kernel(h2_ref, x1_ref, g2_ref, lg_ref, lb_ref, wrt_ref, rb_ref,
                      wg_ref, wu_ref, wd_ref, o_ref, gate_s, acc_s, *, alpha):
    e = pl.program_id(1)
    tm = h2_ref.shape[0]

    @pl.when(e == 0)
    def _():
        gt = _router_gates_t(h2_ref[...], wrt_ref, rb_ref)
        gt = jnp.concatenate([gt, jnp.zeros((LANES - N_EXPERTS, tm), F32)], axis=0)
        g = gt.T
        for ex in range(N_EXPERTS):
            r = EXP_PER_GROUP * (ex % EXP_PER_GROUP) + ex // EXP_PER_GROUP
            gate_s[ex] = jnp.broadcast_to(g[:, r:r + 1], (tm, LANES))
        acc_s[...] = jnp.zeros(acc_s.shape, F32)

    h2 = h2_ref[...]
    hid = jax.nn.silu(_dot(h2, wg_ref[...])) * _dot(h2, wu_ref[...])
    hid = hid * jnp.tile(gate_s[e], (1, hid.shape[1] // LANES))
    acc_s[...] += _dot(hid.astype(BF16), wd_ref[...])

    @pl.when(e == pl.num_programs(1) - 1)
    def _():
        y = alpha * x1_ref[...] + g2_ref[...] * acc_s[...]
        o_ref[...] = _layer_norm(y, lg_ref[...], lb_ref[...])


def _moe_dense(l, h2, x1, mod, ln_g, ln_b, wrt, rb, wg, wu, wd, alpha, tm, tiles_per_seq):
    n, d = x1.shape
    f = wg.shape[-1]
    row = lambda i, e: (i, 0)
    vec = pl.BlockSpec((None, 1, d), lambda i, e: (l, 0, 0))
    return pl.pallas_call(
        functools.partial(_moe_dense_kernel, alpha=alpha),
        out_shape=jax.ShapeDtypeStruct((n, d), F32),
        grid=(n // tm, N_EXPERTS),
        in_specs=[
            pl.BlockSpec((tm, d), row),
            pl.BlockSpec((tm, d), row),
            _mod_spec(l, 5, d, tm, tiles_per_seq),
            vec, vec,
            pl.BlockSpec((N_EXPERTS, d), lambda i, e: (0, 0)),
            pl.BlockSpec((N_EXPERTS, 1), lambda i, e: (0, 0)),
            pl.BlockSpec((None, None, d, f), lambda i, e: (l, e, 0, 0)),
            pl.BlockSpec((None, None, d, f), lambda i, e: (l, e, 0, 0)),
            pl.BlockSpec((None, None, f, d), lambda i, e: (l, e, 0, 0)),
        ],
        out_specs=pl.BlockSpec((tm, d), row),
        scratch_shapes=[
            pltpu.VMEM((N_EXPERTS, tm, LANES), F32),
            pltpu.VMEM((tm, d), F32),
        ],
        compiler_params=_cparams(("arbitrary", "arbitrary")),
        name="moe_dense",
    )(h2, x1, mod, ln_g, ln_b, wrt, rb, wg, wu, wd)


def _rope_tables(pos):
    half = QK_ROPE // 2
    inv = jnp.power(ROPE_THETA, -jnp.arange(half, dtype=F32) * 2.0 / QK_ROPE)
    ang = pos.astype(F32)[:, None] * inv[None, :]
    cos, sin = jnp.cos(ang), jnp.sin(ang)
    pad = jnp.zeros((pos.shape[0], ROPE_PAD - QK_ROPE), F32)
    return (jnp.concatenate([cos, cos, pad], axis=1),
            jnp.concatenate([-sin, sin, pad], axis=1))


def _swap_halves(w):
    half = QK_ROPE // 2
    return jnp.concatenate([w[..., half:], w[..., :half]], axis=-1)


def _pad_last(w, width):
    return jnp.pad(w, [(0, 0)] * (w.ndim - 1) + [(0, width - w.shape[-1])])


def kernel(x_prompt, x_sample, cache_latent, cache_krope, state_conv, page_table, c_prompt, c_sample,
           w_in, q_norm_g, w_uq, kv_norm_g, w_uk, w_uv, conv_w, w_o, ln1_g, ln1_b, ln2_g, ln2_b,
           w_ada, b_ada, w_router, router_bias, w_gate, w_up, w_down):
    batch, seq, d = x_prompt.shape
    dec_b, dec_seq, _ = x_sample.shape
    depth = w_in.shape[0]
    n_pages = page_table.shape[1]
    past_len = n_pages * PAGE_SIZE
    alpha = (2 * depth) ** 0.25
    n_p, n_s = batch * seq, dec_b * dec_seq

    s0, s1, s2, s3, s4 = (Q_LORA, Q_LORA + KV_LORA, Q_LORA + KV_LORA + QK_ROPE,
                          Q_LORA + KV_LORA + QK_ROPE + CONV_CH,
                          Q_LORA + KV_LORA + QK_ROPE + 2 * CONV_CH)
    w_kr = w_in[:, :, s1:s2]
    w_in_r = jnp.concatenate([
        w_in[:, :, :s1], w_in[:, :, s2:],
        _pad_last(w_kr, ROPE_PAD), _pad_last(_swap_halves(w_kr), ROPE_PAD)], axis=-1).astype(BF16)

    w_uq_t = jnp.transpose(w_uq, (0, 2, 1, 3))
    w_uk_t = jnp.transpose(w_uk, (0, 2, 1, 3))
    wuv = jnp.transpose(w_uv, (0, 2, 1, 3)).astype(BF16)
    wuvt = jnp.transpose(w_uv, (0, 2, 3, 1)).astype(BF16)
    kpe_cache_t = jnp.swapaxes(cache_krope, 2, 3)
    qg = q_norm_g[:, :, None]
    w_lat = _fuse_q_weights(w_uq_t[..., :QK_NOPE], w_uk_t, qg)
    w_pe = w_uq_t[..., QK_NOPE:] * qg[:, None] * Q_SCALE
    wa = jnp.concatenate([w_lat, _pad_last(w_pe, ROPE_PAD)], axis=-1).astype(BF16)
    wb = _pad_last(_swap_halves(w_pe), ROPE_PAD).astype(BF16)

    w_o_b = w_o.astype(BF16)
    wg_b, wu_b, wd_b = w_gate.astype(BF16), w_up.astype(BF16), w_down.astype(BF16)
    perm = jnp.arange(N_EXPERTS).reshape(N_GROUPS, EXP_PER_GROUP).T.reshape(-1)
    wrt = w_router.T[perm].astype(BF16)
    rb = router_bias[perm].reshape(N_EXPERTS, 1)

    kv_g = kv_norm_g.reshape(depth, 1, KV_LORA)
    ln1g, ln1b = ln1_g.reshape(depth, 1, d), ln1_b.reshape(depth, 1, d)
    ln2g, ln2b = ln2_g.reshape(depth, 1, d), ln2_b.reshape(depth, 1, d)

    cos_p, sin_p = _rope_tables(jnp.arange(seq))
    cos_s, sin_s = _rope_tables(past_len + jnp.arange(n_s) % dec_seq)

    c_all = jnp.concatenate([jnp.repeat(c_sample, dec_seq, axis=0), c_prompt], axis=0)
    mod = _ada_mod(c_all, w_ada, b_ada)
    mod_p = mod[:, n_s:].reshape(depth, batch, 1, 6 * d)

    tm_p = min(512, seq)
    tps = seq // tm_p
    tq = min(512, seq)
    tk = min(512, seq)
    tm_s = min(256, n_s)

    xp = x_prompt.reshape(n_p, d)
    xs = x_sample.reshape(n_s, d)
    outs = [[] for _ in range(6)]
    for l in range(depth):
        qn, ckv, kpe, kcat, latt, oconv, cst = _premix_prompt(
            l, xp, mod_p, w_in_r, kv_g, conv_w, cos_p, sin_p, batch, seq, tm_p)
        o_att = _attn_prompt(l, qn, kcat, latt, wa, wb, cos_p, sin_p, wuvt, batch, seq, tq, tk)
        x1, h2 = _postmix(l, xp, o_att, oconv, w_o_b, mod_p, ln1g, ln1b, alpha, tm_p, tps)
        xp = _moe_dense(l, h2, x1, mod_p, ln2g, ln2b, wrt, rb, wg_b, wu_b, wd_b, alpha, tm_p, tps)
        outs[0].append(ckv.reshape(batch, seq, KV_LORA))
        outs[1].append(kpe.reshape(batch, seq, QK_ROPE))
        outs[2].append(cst)
        st = state_conv[l]
        pad_n = dec_seq - 1
        e1 = jnp.pad(st[:, 1:2], ((0, 0), (0, pad_n), (0, 0))).reshape(n_s, CONV_CH)
        e2 = jnp.pad(st, ((0, 0), (0, dec_seq - 2), (0, 0))).reshape(n_s, CONV_CH)
        qs, ckv_s, kpe_s, kcat_s, oconv_s, u_s = _premix_sample(
            l, xs, mod, w_in_r, kv_g, conv_w, cos_s, sin_s, e1, e2, wa, wb, dec_seq, tm_s)
        o_lat = _attn_sample(l, page_table, qs, kcat_s, cache_latent, kpe_cache_t, dec_seq)
        o_att_s = _uv_sample(l, o_lat, wuv, dec_seq, tm_s // dec_seq)
        x1s, h2s = _postmix(l, xs, o_att_s, oconv_s, w_o_b, mod, ln1g, ln1b, alpha, tm_s, None)
        xs = _moe_dense(l, h2s, x1s, mod, ln2g, ln2b, wrt, rb, wg_b, wu_b, wd_b, alpha, tm_s, None)
        outs[3].append(ckv_s.reshape(dec_b, dec_seq, KV_LORA))
        outs[4].append(kpe_s.reshape(dec_b, dec_seq, QK_ROPE))
        outs[5].append(u_s.reshape(dec_b, dec_seq, CONV_CH)[:, dec_seq - (CONV_K - 1):])
    return (xp.reshape(batch, seq, d), xs.reshape(dec_b, dec_seq, d),
            jnp.stack(outs[0]), jnp.stack(outs[1]), jnp.stack(outs[2]),
            jnp.stack(outs[3]), jnp.stack(outs[4]), jnp.stack(outs[5]))
```

```python
import functools

import jax
import jax.numpy as jnp
from jax import lax
from jax.experimental import pallas as pl
from jax.experimental.pallas import tpu as pltpu

N_HEADS = 8
QK_NOPE = 64
QK_ROPE = 32
V_HEAD = 64
Q_LORA = 256
KV_LORA = 256
ROPE_THETA = 10000.0
SM_SCALE = (QK_NOPE + QK_ROPE) ** -0.5
LOG2_E = 1.4426950408889634
Q_SCALE = SM_SCALE * LOG2_E
PAGE_SIZE = 128
CONV_K = 3
N_EXPERTS = 16
N_GROUPS = 4
EXP_PER_GROUP = N_EXPERTS // N_GROUPS
LN_EPS = 1e-5
RMS_EPS = 1e-6

LANES = 128
ROPE_PAD = LANES
QK_W = KV_LORA + ROPE_PAD
NEG_BIG = -1e30
VMEM_LIMIT = 56 * 1024 * 1024

BF16 = jnp.bfloat16
F32 = jnp.float32


def _cparams(sem):
    return pltpu.CompilerParams(dimension_semantics=sem, vmem_limit_bytes=VMEM_LIMIT)


def _dot(a, b):
    return jnp.dot(a, b, preferred_element_type=F32)


def _dot_nt(a, b):
    return lax.dot_general(a, b, (((1,), (1,)), ((), ())), preferred_element_type=F32)


def _layer_norm(y, g, b):
    mu = jnp.mean(y, axis=-1, keepdims=True)
    d = y - mu
    var = jnp.mean(d * d, axis=-1, keepdims=True)
    return d * lax.rsqrt(var + LN_EPS) * g + b


def _rms_scale(v):
    return lax.rsqrt(jnp.mean(v * v, axis=-1, keepdims=True) + RMS_EPS)


def _ada_kernel(c_ref, w_ref, b_ref, o_ref):
    o_ref[...] = _dot(c_ref[...], w_ref[...].astype(BF16)) + b_ref[...]


def _ada_mod(c_all, w_ada, b_ada):
    depth, d, d6 = w_ada.shape
    n = c_all.shape[0]
    tn = 512
    return pl.pallas_call(
        _ada_kernel,
        out_shape=jax.ShapeDtypeStruct((depth, n, d6), F32),
        grid=(depth, d6 // tn),
        in_specs=[
            pl.BlockSpec((n, d), lambda l, j: (0, 0)),
            pl.BlockSpec((None, d, tn), lambda l, j: (l, 0, j)),
            pl.BlockSpec((None, 1, tn), lambda l, j: (l, 0, j)),
        ],
        out_specs=pl.BlockSpec((None, n, tn), lambda l, j: (l, 0, j)),
        compiler_params=_cparams(("arbitrary", "arbitrary")),
        name="ada_mod",
    )(c_all.astype(BF16), w_ada, b_ada.reshape(depth, 1, d6))


def _qfuse_kernel(uq_ref, uk_ref, g_ref, o_ref):
    a = uq_ref[...] * g_ref[...] * Q_SCALE
    o_ref[...] = lax.dot_general(a, uk_ref[...], (((1,), (1,)), ((), ())),
                                 preferred_element_type=F32,
                                 precision=lax.Precision.HIGHEST)


def _fuse_q_weights(w_uq_nope_t, w_uk_t, q_norm_g):
    depth = w_uq_nope_t.shape[0]
    return pl.pallas_call(
        _qfuse_kernel,
        out_shape=jax.ShapeDtypeStruct((depth, N_HEADS, Q_LORA, KV_LORA), F32),
        grid=(depth, N_HEADS),
        in_specs=[
            pl.BlockSpec((None, None, Q_LORA, QK_NOPE), lambda l, h: (l, h, 0, 0)),
            pl.BlockSpec((None, None, KV_LORA, QK_NOPE), lambda l, h: (l, h, 0, 0)),
            pl.BlockSpec((None, Q_LORA, 1), lambda l, h: (l, 0, 0)),
        ],
        out_specs=pl.BlockSpec((None, None, Q_LORA, KV_LORA), lambda l, h: (l, h, 0, 0)),
        compiler_params=_cparams(("arbitrary", "arbitrary")),
        name="fuse_q_weights",
    )(w_uq_nope_t, w_uk_t, q_norm_g)


C_QA = 0
C_KV = C_QA + Q_LORA
C_GB = C_KV + KV_LORA
CONV_CH = 512
C_GC = C_GB + CONV_CH
C_V = C_GC + CONV_CH
C_KR = C_V + CONV_CH
C_KRS = C_KR + ROPE_PAD
IN_COLS_R = C_KRS + ROPE_PAD


def _premix_common(x_ref, sh_ref, sc_ref, w_ref, kvg_ref, cw_ref, cos_ref, sin_ref):
    h = (x_ref[...] * (1.0 + sc_ref[...]) + sh_ref[...]).astype(BF16)
    q_a = _dot(h, w_ref[:, C_QA:C_KV])
    qn = (q_a * _rms_scale(q_a)).astype(BF16)
    kv_a = _dot(h, w_ref[:, C_KV:C_GB])
    ckv = kv_a * _rms_scale(kv_a) * kvg_ref[...]
    kr = _dot(h, w_ref[:, C_KR:C_KRS])
    krs = _dot(h, w_ref[:, C_KRS:IN_COLS_R])
    kpe = kr * cos_ref[...] + krs * sin_ref[...]
    gate_b = _dot(h, w_ref[:, C_GB:C_GC])
    u = _dot(h, w_ref[:, C_GC:C_V]) * _dot(h, w_ref[:, C_V:C_KR])
    return qn, ckv, kpe, gate_b, u


def _premix_prompt_kernel(x_ref, sh_ref, sc_ref, w_ref, kvg_ref, cw_ref, cos_ref, sin_ref,
                          qn_ref, ckv_ref, kpe_ref, kcat_ref, latt_ref, oconv_ref, cst_ref,
                          ubuf, *, tiles_per_seq):
    tm = x_ref.shape[0]
    qn, ckv, kpe, gate_b, u = _premix_common(x_ref, sh_ref, sc_ref, w_ref, kvg_ref, cw_ref,
                                             cos_ref, sin_ref)
    qn_ref[...] = qn
    ckv_ref[...] = ckv
    kpe_ref[...] = kpe[:, :QK_ROPE]
    kcat_ref[:, :KV_LORA] = ckv.astype(BF16)
    kcat_ref[:, KV_LORA:] = kpe.astype(BF16)
    latt_ref[...] = ckv.T.astype(BF16)

    @pl.when(pl.program_id(0) % tiles_per_seq == 0)
    def _():
        ubuf[0:8, :] = jnp.zeros((8, CONV_CH), F32)

    ubuf[8:8 + tm, :] = u
    y = (ubuf[6:6 + tm, :] * cw_ref[0:1, :] + ubuf[7:7 + tm, :] * cw_ref[1:2, :]
         + u * cw_ref[2:3, :])
    oconv_ref[...] = (gate_b * y).astype(BF16)
    cst_ref[...] = u[tm - (CONV_K - 1):, :]
    ubuf[0:8, :] = u[tm - 8:, :]


def _premix_sample_kernel(x_ref, sh_ref, sc_ref, w_ref, kvg_ref, cw_ref, cos_ref, sin_ref,
                          e1_ref, e2_ref, wa_ref, wb_ref,
                          qs_ref, ckv_ref, kpe_ref, kcat_ref, oconv_ref, u_ref,
                          ubuf, *, dec_seq):
    tm = x_ref.shape[0]
    qn, ckv, kpe, gate_b, u = _premix_common(x_ref, sh_ref, sc_ref, w_ref, kvg_ref, cw_ref,
                                             cos_ref, sin_ref)
    ckv_ref[...] = ckv
    kpe_ref[...] = kpe[:, :QK_ROPE]
    kcat_ref[:, :KV_LORA] = ckv
    kcat_ref[:, KV_LORA:] = kpe
    u_ref[...] = u

    ubuf[0:8, :] = jnp.zeros((8, CONV_CH), F32)
    ubuf[8:8 + tm, :] = u
    t = lax.broadcasted_iota(jnp.int32, (tm, CONV_CH), 0) % dec_seq
    u1 = jnp.where(t >= 1, ubuf[7:7 + tm, :], e1_ref[...])
    u2 = jnp.where(t >= 2, ubuf[6:6 + tm, :], e2_ref[...])
    y = u2 * cw_ref[0:1, :] + u1 * cw_ref[1:2, :] + u * cw_ref[2:3, :]
    oconv_ref[...] = (gate_b * y).astype(BF16)

    nb = tm // dec_seq
    for h in range(N_HEADS):
        qa = _dot(qn, wa_ref[h])
        qr = qa[:, KV_LORA:] * cos_ref[...] + _dot(qn, wb_ref[h]) * sin_ref[...]
        qs_ref[:, h * dec_seq:(h + 1) * dec_seq, :KV_LORA] = (
            qa[:, :KV_LORA].reshape(nb, dec_seq, KV_LORA))
        qs_ref[:, h * dec_seq:(h + 1) * dec_seq, KV_LORA:] = qr.reshape(nb, dec_seq, ROPE_PAD)


def _premix_prompt(l, x, mod_p, w_in_r, kv_g, conv_w, cos_t, sin_t, batch, seq, tm):
    n, d = x.shape
    tiles_per_seq = seq // tm
    kern = functools.partial(_premix_prompt_kernel, tiles_per_seq=tiles_per_seq)
    row = lambda i: (i, 0)
    outs = pl.pallas_call(
        kern,
        out_shape=(
            jax.ShapeDtypeStruct((n, Q_LORA), BF16),
            jax.ShapeDtypeStruct((n, KV_LORA), F32),
            jax.ShapeDtypeStruct((n, QK_ROPE), F32),
            jax.ShapeDtypeStruct((n, QK_W), BF16),
            jax.ShapeDtypeStruct((KV_LORA, n), BF16),
            jax.ShapeDtypeStruct((n, CONV_CH), BF16),
            jax.ShapeDtypeStruct((batch, CONV_K - 1, CONV_CH), F32),
        ),
        grid=(n // tm,),
        in_specs=[
            pl.BlockSpec((tm, d), row),
            pl.BlockSpec((None, None, 1, d), lambda i: (l, i // tiles_per_seq, 0, 0)),
            pl.BlockSpec((None, None, 1, d), lambda i: (l, i // tiles_per_seq, 0, 1)),
            pl.BlockSpec((None, d, IN_COLS_R), lambda i: (l, 0, 0)),
            pl.BlockSpec((None, 1, KV_LORA), lambda i: (l, 0, 0)),
            pl.BlockSpec((None, CONV_K, CONV_CH), lambda i: (l, 0, 0)),
            pl.BlockSpec((tm, ROPE_PAD), lambda i: (i % tiles_per_seq, 0)),
            pl.BlockSpec((tm, ROPE_PAD), lambda i: (i % tiles_per_seq, 0)),
        ],
        out_specs=(
            pl.BlockSpec((tm, Q_LORA), row),
            pl.BlockSpec((tm, KV_LORA), row),
            pl.BlockSpec((tm, QK_ROPE), row),
            pl.BlockSpec((tm, QK_W), row),
            pl.BlockSpec((KV_LORA, tm), lambda i: (0, i)),
            pl.BlockSpec((tm, CONV_CH), row),
            pl.BlockSpec((None, CONV_K - 1, CONV_CH), lambda i: (i // tiles_per_seq, 0, 0)),
        ),
        scratch_shapes=[pltpu.VMEM((tm + 8, CONV_CH), F32)],
        compiler_params=_cparams(("arbitrary",)),
        name="premix_prompt",
    )(x, mod_p, mod_p, w_in_r, kv_g, conv_w, cos_t, sin_t)
    return outs


def _premix_sample(l, x, mod, w_in_r, kv_g, conv_w, cos_t, sin_t, e1, e2, wa, wb, dec_seq, tm):
    n, d = x.shape
    kern = functools.partial(_premix_sample_kernel, dec_seq=dec_seq)
    row = lambda i: (i, 0)
    nb = tm // dec_seq
    hq = N_HEADS * dec_seq
    outs = pl.pallas_call(
        kern,
        out_shape=(
            jax.ShapeDtypeStruct((n // dec_seq, hq, QK_W), F32),
            jax.ShapeDtypeStruct((n, KV_LORA), F32),
            jax.ShapeDtypeStruct((n, QK_ROPE), F32),
            jax.ShapeDtypeStruct((n, QK_W), F32),
            jax.ShapeDtypeStruct((n, CONV_CH), BF16),
            jax.ShapeDtypeStruct((n, CONV_CH), F32),
        ),
        grid=(n // tm,),
        in_specs=[
            pl.BlockSpec((tm, d), row),
            pl.BlockSpec((None, tm, d), lambda i: (l, i, 0)),
            pl.BlockSpec((None, tm, d), lambda i: (l, i, 1)),
            pl.BlockSpec((None, d, IN_COLS_R), lambda i: (l, 0, 0)),
            pl.BlockSpec((None, 1, KV_LORA), lambda i: (l, 0, 0)),
            pl.BlockSpec((None, CONV_K, CONV_CH), lambda i: (l, 0, 0)),
            pl.BlockSpec((tm, ROPE_PAD), row),
            pl.BlockSpec((tm, ROPE_PAD), row),
            pl.BlockSpec((tm, CONV_CH), row),
            pl.BlockSpec((tm, CONV_CH), row),
            pl.BlockSpec((None, N_HEADS, Q_LORA, QK_W), lambda i: (l, 0, 0, 0)),
            pl.BlockSpec((None, N_HEADS, Q_LORA, ROPE_PAD), lambda i: (l, 0, 0, 0)),
        ],
        out_specs=(
            pl.BlockSpec((nb, hq, QK_W), lambda i: (i, 0, 0)),
            pl.BlockSpec((tm, KV_LORA), row),
            pl.BlockSpec((tm, QK_ROPE), row),
            pl.BlockSpec((tm, QK_W), row),
            pl.BlockSpec((tm, CONV_CH), row),
            pl.BlockSpec((tm, CONV_CH), row),
        ),
        scratch_shapes=[pltpu.VMEM((tm + 8, CONV_CH), F32)],
        compiler_params=_cparams(("arbitrary",)),
        name="premix_sample",
    )(x, mod, mod, w_in_r, kv_g, conv_w, cos_t, sin_t, e1, e2, wa, wb)
    return outs


def _attn_prompt_kernel(qn_ref, k_ref, vt_ref, wa_ref, wb_ref, cos_ref, sin_ref, wuvt_ref, o_ref,
                        q_s, m_s, l_s, acc_s, *, tq, tk):
    qi = pl.program_id(1)
    qn = qn_ref[...]
    for h in range(N_HEADS):
        qa = _dot(qn, wa_ref[h])
        qr = qa[:, KV_LORA:] * cos_ref[...] + _dot(qn, wb_ref[h]) * sin_ref[...]
        q_s[h, :, :KV_LORA] = qa[:, :KV_LORA].astype(BF16)
        q_s[h, :, KV_LORA:] = qr.astype(BF16)
    m_s[...] = jnp.full(m_s.shape, NEG_BIG, F32)
    l_s[...] = jnp.zeros(l_s.shape, F32)
    acc_s[...] = jnp.zeros(acc_s.shape, F32)

    def step(ki, masked):
        start = pl.multiple_of(ki * tk, tk)
        kt = k_ref[pl.ds(start, tk), :]
        vt = vt_ref[:, pl.ds(start, tk)]
        if masked:
            k_pos = start + lax.broadcasted_iota(jnp.int32, (tk, tq), 0)
            q_pos = qi * tq + lax.broadcasted_iota(jnp.int32, (tk, tq), 1)
            keep = k_pos <= q_pos
        for h in range(N_HEADS):
            s = _dot_nt(kt, q_s[h])
            if masked:
                s = jnp.where(keep, s, NEG_BIG)
            m_old = m_s[h]
            m_new = jnp.maximum(m_old, jnp.max(s, axis=0, keepdims=True))
            p = jnp.exp2(s - m_new)
            alpha = jnp.exp2(m_old - m_new)
            l_s[h] = alpha * l_s[h] + jnp.sum(p, axis=0, keepdims=True)
            acc_s[h] = alpha * acc_s[h] + _dot(vt, p.astype(BF16))
            m_s[h] = m_new

    n_full = (qi * tq) // tk
    lax.fori_loop(0, n_full, lambda ki, c: (step(ki, False), c)[1], 0)
    n_diag = tq // tk if tq > tk else 1
    for j in range(n_diag):
        step(n_full + j, True)

    heads = []
    for h in range(N_HEADS):
        o_t = (acc_s[h] * (1.0 / l_s[h])).astype(BF16)
        heads.append(_dot(wuvt_ref[h], o_t))
    o_ref[...] = jnp.concatenate(heads, axis=0).T.astype(BF16)


def _attn_prompt(l, qn, kcat, latt, wa, wb, cos_t, sin_t, wuvt, batch, seq, tq, tk):
    n = qn.shape[0]
    nq = seq // tq
    kern = functools.partial(_attn_prompt_kernel, tq=tq, tk=tk)
    return pl.pallas_call(
        kern,
        out_shape=jax.ShapeDtypeStruct((n, N_HEADS * V_HEAD), BF16),
        grid=(batch, nq),
        in_specs=[
            pl.BlockSpec((tq, Q_LORA), lambda b, i: (b * nq + i, 0)),
            pl.BlockSpec((seq, QK_W), lambda b, i: (b, 0)),
            pl.BlockSpec((KV_LORA, seq), lambda b, i: (0, b)),
            pl.BlockSpec((None, N_HEADS, Q_LORA, QK_W), lambda b, i: (l, 0, 0, 0)),
            pl.BlockSpec((None, N_HEADS, Q_LORA, ROPE_PAD), lambda b, i: (l, 0, 0, 0)),
            pl.BlockSpec((tq, ROPE_PAD), lambda b, i: (i, 0)),
            pl.BlockSpec((tq, ROPE_PAD), lambda b, i: (i, 0)),
            pl.BlockSpec((None, N_HEADS, V_HEAD, KV_LORA), lambda b, i: (l, 0, 0, 0)),
        ],
        out_specs=pl.BlockSpec((tq, N_HEADS * V_HEAD), lambda b, i: (b * nq + i, 0)),
        scratch_shapes=[
            pltpu.VMEM((N_HEADS, tq, QK_W), BF16),
            pltpu.VMEM((N_HEADS, 1, tq), F32),
            pltpu.VMEM((N_HEADS, 1, tq), F32),
            pltpu.VMEM((N_HEADS, KV_LORA, tq), F32),
        ],
        compiler_params=_cparams(("arbitrary", "arbitrary")),
        name="attn_prompt",
    )(qn, kcat, latt, wa, wb, cos_t, sin_t, wuvt)


def _attn_sample_kernel(pt_ref, qs_ref, knew_ref, lat_hbm, kpe_hbm, o_ref,
                        lat_buf, kpe_buf, s_buf, sems, *, layer, n_pages, dec_seq, chunk):
    b = pl.program_id(0)
    nb = pl.num_programs(0)
    hq = N_HEADS * dec_seq
    past = n_pages * PAGE_SIZE

    def page_copies(seq_idx, slot, j):
        page = pt_ref[seq_idx, j]
        dst = pl.ds(j * PAGE_SIZE, PAGE_SIZE)
        return (
            pltpu.make_async_copy(lat_hbm.at[layer, page], lat_buf.at[slot, dst, :],
                                  sems.at[slot, 0]),
            pltpu.make_async_copy(kpe_hbm.at[layer, page], kpe_buf.at[slot, j],
                                  sems.at[slot, 1]),
        )

    def start_all(seq_idx, slot):
        def body(j, c):
            for cp in page_copies(seq_idx, slot, j):
                cp.start()
            return c
        lax.fori_loop(0, n_pages, body, 0)

    def wait_all(seq_idx, slot):
        def body(j, c):
            for cp in page_copies(seq_idx, slot, j):
                cp.wait()
            return c
        lax.fori_loop(0, n_pages, body, 0)

    slot = b % 2

    @pl.when(b == 0)
    def _():
        start_all(b, slot)

    @pl.when(b + 1 < nb)
    def _():
        start_all(b + 1, 1 - slot)

    wait_all(b, slot)

    q = qs_ref[...]
    q_lat = q[:, :KV_LORA].astype(BF16)
    q_pe = q[:, KV_LORA:KV_LORA + QK_ROPE].astype(BF16)
    n_chunks = past // chunk
    pages_per_chunk = chunk // PAGE_SIZE
    for c in range(n_chunks):
        rows = pl.ds(c * chunk, chunk)
        lat_c = lat_buf[slot, rows, :].astype(BF16)
        s_pe = jnp.concatenate(
            [_dot(q_pe, kpe_buf[slot, c * pages_per_chunk + j].astype(BF16))
             for j in range(pages_per_chunk)], axis=1)
        s_buf[:, c * chunk:(c + 1) * chunk] = _dot_nt(q_lat, lat_c) + s_pe

    knew = knew_ref[...]
    s_new = _dot_nt(q, knew)
    t_q = lax.broadcasted_iota(jnp.int32, (hq, dec_seq), 0) % dec_seq
    t_k = lax.broadcasted_iota(jnp.int32, (hq, dec_seq), 1)
    s_new = jnp.where(t_k <= t_q, s_new, NEG_BIG)

    s_past = s_buf[...]
    m = jnp.maximum(jnp.max(s_past, axis=-1, keepdims=True),
                    jnp.max(s_new, axis=-1, keepdims=True))
    p_new = jnp.exp2(s_new - m)
    p_past = jnp.exp2(s_past - m)
    denom = jnp.sum(p_past, axis=-1, keepdims=True) + jnp.sum(p_new, axis=-1, keepdims=True)
    s_buf[...] = p_past
    acc = _dot(p_new, knew[:, :KV_LORA])
    for c in range(n_chunks):
        rows = pl.ds(c * chunk, chunk)
        lat_c = lat_buf[slot, rows, :].astype(BF16)
        acc = acc + _dot(s_buf[:, c * chunk:(c + 1) * chunk].astype(BF16), lat_c)
    o_ref[...] = acc / denom


def _attn_sample(layer, page_table, qs, kcat_new, cache_latent, cache_krope, dec_seq):
    nb, hq, _ = qs.shape
    n_pages = page_table.shape[1]
    past = n_pages * PAGE_SIZE
    chunk = min(past, 1024)
    kern = functools.partial(_attn_sample_kernel, layer=layer, n_pages=n_pages,
                             dec_seq=dec_seq, chunk=chunk)
    grid_spec = pltpu.PrefetchScalarGridSpec(
        num_scalar_prefetch=1,
        grid=(nb,),
        in_specs=[
            pl.BlockSpec((None, hq, QK_W), lambda b, pt: (b, 0, 0)),
            pl.BlockSpec((dec_seq, QK_W), lambda b, pt: (b, 0)),
            pl.BlockSpec(memory_space=pl.ANY),
            pl.BlockSpec(memory_space=pl.ANY),
        ],
        out_specs=pl.BlockSpec((None, hq, KV_LORA), lambda b, pt: (b, 0, 0)),
        scratch_shapes=[
            pltpu.VMEM((2, past, KV_LORA), F32),
            pltpu.VMEM((2, n_pages, QK_ROPE, PAGE_SIZE), F32),
            pltpu.VMEM((hq, past), F32),
            pltpu.SemaphoreType.DMA((2, 2)),
        ],
    )
    return pl.pallas_call(
        kern,
        out_shape=jax.ShapeDtypeStruct((nb, hq, KV_LORA), F32),
        grid_spec=grid_spec,
        compiler_params=_cparams(("arbitrary",)),
        name="attn_sample",
    )(page_table, qs, kcat_new, cache_latent, cache_krope)


def _uv_sample_kernel(ol_ref, wuv_ref, o_ref, *, dec_seq):
    nbk = ol_ref.shape[0]
    heads = []
    for h in range(N_HEADS):
        o_h = ol_ref[:, h * dec_seq:(h + 1) * dec_seq, :].reshape(nbk * dec_seq, KV_LORA)
        heads.append(_dot(o_h.astype(BF16), wuv_ref[h]))
    o_ref[...] = jnp.concatenate(heads, axis=1).astype(BF16)


def _uv_sample(l, o_lat, wuv, dec_seq, nbk):
    nb, hq, _ = o_lat.shape
    kern = functools.partial(_uv_sample_kernel, dec_seq=dec_seq)
    return pl.pallas_call(
        kern,
        out_shape=jax.ShapeDtypeStruct((nb * dec_seq, N_HEADS * V_HEAD), BF16),
        grid=(nb // nbk,),
        in_specs=[
            pl.BlockSpec((nbk, hq, KV_LORA), lambda i: (i, 0, 0)),
            pl.BlockSpec((None, N_HEADS, KV_LORA, V_HEAD), lambda i: (l, 0, 0, 0)),
        ],
        out_specs=pl.BlockSpec((nbk * dec_seq, N_HEADS * V_HEAD), lambda i: (i, 0)),
        compiler_params=_cparams(("arbitrary",)),
        name="uv_sample",
    )(o_lat, wuv)


def _postmix_kernel(x_ref, oa_ref, oc_ref, wo_ref, g1_ref, lg_ref, lb_ref, x1_ref, *, alpha):
    half = oa_ref.shape[1]
    mix = _dot(oa_ref[...], wo_ref[:half, :]) + _dot(oc_ref[...], wo_ref[half:, :])
    x1_ref[...] = _layer_norm(alpha * x_ref[...] + g1_ref[...] * mix, lg_ref[...], lb_ref[...])


def _mod_spec(l, chunk, d, tm, tiles_per_seq):
    if tiles_per_seq is None:
        return pl.BlockSpec((None, tm, d), lambda i, *_: (l, i, chunk))
    return pl.BlockSpec((None, None, 1, d), lambda i, *_: (l, i // tiles_per_seq, 0, chunk))


def _postmix(l, x, o_att, o_conv, w_o, mod, ln_g, ln_b, alpha, tm, tiles_per_seq):
    n, d = x.shape
    half = o_att.shape[1]
    row = lambda i: (i, 0)
    vec = pl.BlockSpec((None, 1, d), lambda i: (l, 0, 0))
    return pl.pallas_call(
        functools.partial(_postmix_kernel, alpha=alpha),
        out_shape=jax.ShapeDtypeStruct((n, d), F32),
        grid=(n // tm,),
        in_specs=[
            pl.BlockSpec((tm, d), row),
            pl.BlockSpec((tm, half), row),
            pl.BlockSpec((tm, half), row),
            pl.BlockSpec((None, 2 * half, d), lambda i: (l, 0, 0)),
            _mod_spec(l, 2, d, tm, tiles_per_seq),
            vec, vec,
        ],
        out_specs=pl.BlockSpec((tm, d), row),
        compiler_params=_cparams(("arbitrary",)),
        name="postmix",
    )(x, o_att, o_conv, w_o, mod, ln_g, ln_b)


MOE_ROW_TILE = 256


def _pack_pair(lo, hi):
    lo_b = lax.bitcast_convert_type(lo.astype(BF16).astype(F32), jnp.uint32)
    hi_b = lax.bitcast_convert_type(hi.astype(BF16).astype(F32), jnp.uint32)
    return jnp.bitwise_or(hi_b, lax.shift_right_logical(lo_b, jnp.uint32(16)))


def _unpack_pair(p):
    lo = lax.bitcast_convert_type(lax.shift_left(p, jnp.uint32(16)), F32)
    hi = lax.bitcast_convert_type(jnp.bitwise_and(p, jnp.uint32(0xFFFF0000)), F32)
    return lo, hi


def _route_top2(h2, wr_ref, rb_ref):
    tb = h2.shape[0]
    h_hi = h2.astype(BF16)
    h_lo = (h2 - h_hi.astype(F32)).astype(BF16)
    w = wr_ref[...]
    part = _dot_nt(w, h_hi)
    logits = part[:N_EXPERTS] + part[N_EXPERTS:] + _dot_nt(w[:N_EXPERTS], h_lo)
    scores = jax.nn.sigmoid(logits)
    sel = scores + rb_ref[...]
    a = [sel[EXP_PER_GROUP * j:EXP_PER_GROUP * (j + 1), :] for j in range(EXP_PER_GROUP)]
    sc = [scores[EXP_PER_GROUP * j:EXP_PER_GROUP * (j + 1), :] for j in range(EXP_PER_GROUP)]

    def first_max(vals):
        mx = functools.reduce(jnp.maximum, vals)
        taken = jnp.zeros(vals[0].shape, jnp.bool_)
        firsts = []
        for v in vals:
            hit = jnp.logical_and(v == mx, jnp.logical_not(taken))
            firsts.append(hit)
            taken = jnp.logical_or(taken, hit)
        return mx, firsts

    m1, is1 = first_max(a)
    rest = [jnp.where(f, -jnp.inf, v) for f, v in zip(is1, a)]
    m2, is2 = first_max(rest)
    grp = m1 + m2
    rows = [grp[g:g + 1, :] for g in range(N_GROUPS)]
    gmax = functools.reduce(jnp.maximum, rows)
    gidx = jnp.full((1, tb), N_GROUPS - 1, jnp.int32)
    for g in range(N_GROUPS - 2, -1, -1):
        gidx = jnp.where(rows[g] == gmax, g, gidx)
    g_iota = lax.broadcasted_iota(jnp.int32, (N_GROUPS, tb), 0)
    in_group = g_iota == gidx

    def pick(flags, vals):
        tot = functools.reduce(
            jnp.add, [jnp.where(jnp.logical_and(in_group, f), v, 0.0) for f, v in zip(flags, vals)])
        return jnp.sum(tot, axis=0, keepdims=True)

    eids = [(EXP_PER_GROUP * g_iota + j).astype(F32) for j in range(EXP_PER_GROUP)]
    e1, e2 = pick(is1, eids), pick(is2, eids)
    s1, s2 = pick(is1, sc), pick(is2, sc)
    inv = 1.0 / (s1 + s2)
    return e1, e2, s1 * inv, s2 * inv


def _plan_kernel(x1_ref, sh2_ref, sc2_ref, wr_ref, rb_ref, tri_ref, mmat_ref,
                 h2p_ref, pos_ref, wts_ref, meta_ref, *, rt):
    tb, d = x1_ref.shape
    half = d // 2
    h2 = x1_ref[...] * (1.0 + sc2_ref[...]) + sh2_ref[...]
    h2p_ref[...] = _pack_pair(h2[:, :half], h2[:, half:])
    e1, e2, w1, w2 = _route_top2(h2, wr_ref, rb_ref)

    e_iota = lax.broadcasted_iota(jnp.int32, (N_EXPERTS, tb), 0).astype(F32)
    h1 = (e_iota == e1).astype(F32)
    h2m = (e_iota == e2).astype(F32)
    hcat = jnp.concatenate([h1, h2m], axis=0).astype(BF16)
    pref = _dot(hcat, tri_ref[...])
    c1 = jnp.sum(h1, axis=1, keepdims=True)
    cnt = c1 + jnp.sum(h2m, axis=1, keepdims=True)
    nt = jnp.floor((cnt + (rt - 1)) * (1.0 / rt))
    r_i = lax.broadcasted_iota(jnp.int32, (N_EXPERTS, N_EXPERTS), 0)
    c_i = lax.broadcasted_iota(jnp.int32, (N_EXPERTS, N_EXPERTS), 1)
    ltri = (c_i < r_i).astype(BF16)
    toff = _dot(ltri, jnp.broadcast_to(nt, (N_EXPERTS, LANES)).astype(BF16))[:, 0:1]
    base1 = toff * rt
    base2 = base1 + c1
    pos1 = jnp.sum(h1 * (base1 + pref[:N_EXPERTS]), axis=0, keepdims=True)
    pos2 = jnp.sum(h2m * (base2 + pref[N_EXPERTS:]), axis=0, keepdims=True)
    pos_ref[...] = jnp.concatenate([pos1, pos2], axis=1).astype(jnp.int32)
    wts_ref[...] = jnp.concatenate([w1, w2], axis=1)
    cnt_t = _dot_nt(jnp.ones((8, tb), BF16), (h1 + h2m).astype(BF16))
    nt_t = jnp.floor((cnt_t + (rt - 1)) * (1.0 / rt)).astype(BF16)
    meta_ref[...] = _dot(nt_t, mmat_ref[...])[0:1, :].astype(jnp.int32)


def _plan(l, x1, mod, wr, rb, tri, mmat, tb, tiles_per_seq):
    n, d = x1.shape
    nblk = n // tb
    row = lambda i: (i, 0)
    blk3 = lambda i: (i, 0, 0)
    return pl.pallas_call(
        functools.partial(_plan_kernel, rt=MOE_ROW_TILE),
        out_shape=(
            jax.ShapeDtypeStruct((n, d // 2), jnp.uint32),
            jax.ShapeDtypeStruct((nblk, 1, 2 * tb), jnp.int32),
            jax.ShapeDtypeStruct((nblk, 1, 2 * tb), F32),
            jax.ShapeDtypeStruct((nblk, 1, LANES), jnp.int32),
        ),
        grid=(nblk,),
        in_specs=[
            pl.BlockSpec((tb, d), row),
            _mod_spec(l, 3, d, tb, tiles_per_seq),
            _mod_spec(l, 4, d, tb, tiles_per_seq),
            pl.BlockSpec((2 * N_EXPERTS, d), lambda i: (0, 0)),
            pl.BlockSpec((N_EXPERTS, 1), lambda i: (0, 0)),
            pl.BlockSpec((tb, tb), lambda i: (0, 0)),
            pl.BlockSpec((N_EXPERTS, LANES), lambda i: (0, 0)),
        ],
        out_specs=(
            pl.BlockSpec((tb, d // 2), row),
            pl.BlockSpec((None, 1, 2 * tb), blk3),
            pl.BlockSpec((None, 1, 2 * tb), blk3),
            pl.BlockSpec((None, 1, LANES), blk3),
        ),
        compiler_params=_cparams(("arbitrary",)),
        name="moe_plan",
    )(x1, mod, mod, wr, rb, tri, mmat)


def _moe_kernel(pos_ref, wts_ref, meta_ref, h2p_ref, x1_ref, g2_ref, lg_ref, lb_ref,
                wg_ref, wu_ref, wd_ref, o_ref, xs, *, alpha, rt):
    e = pl.program_id(1)
    tb, d = x1_ref.shape
    half = d // 2

    @pl.when(e == 0)
    def _():
        xs[...] = jnp.zeros(xs.shape, jnp.uint32)

        def scatter(t, c):
            row = h2p_ref[pl.ds(t, 1), :]
            xs[pl.ds(pos_ref[0, t], 1), :] = row
            xs[pl.ds(pos_ref[0, tb + t], 1), :] = row
            return c
        lax.fori_loop(0, tb, scatter, 0, unroll=8)

    toff = meta_ref[0, e]
    n_tiles = meta_ref[0, N_EXPERTS + e]

    def tile(j, c):
        r0 = pl.multiple_of((toff + j) * rt, rt)
        lo, hi = _unpack_pair(xs[pl.ds(r0, rt), :])
        lo, hi = lo.astype(BF16), hi.astype(BF16)
        gate = _dot(lo, wg_ref[:half, :]) + _dot(hi, wg_ref[half:, :])
        up = _dot(lo, wu_ref[:half, :]) + _dot(hi, wu_ref[half:, :])
        y = _dot((jax.nn.silu(gate) * up).astype(BF16), wd_ref[...])
        xs[pl.ds(r0, rt), :] = _pack_pair(y[:, :half], y[:, half:])
        return c
    lax.fori_loop(0, n_tiles, tile, 0)

    @pl.when(e == pl.num_programs(1) - 1)
    def _():
        def gather(t, c):
            lo1, hi1 = _unpack_pair(xs[pl.ds(pos_ref[0, t], 1), :])
            lo2, hi2 = _unpack_pair(xs[pl.ds(pos_ref[0, tb + t], 1), :])
            w1, w2 = wts_ref[0, t], wts_ref[0, tb + t]
            o_ref[pl.ds(t, 1), :half] = w1 * lo1 + w2 * lo2
            o_ref[pl.ds(t, 1), half:] = w1 * hi1 + w2 * hi2
            return c
        lax.fori_loop(0, tb, gather, 0, unroll=8)
        y = alpha * x1_ref[...] + g2_ref[...] * o_ref[...]
        o_ref[...] = _layer_norm(y, lg_ref[...], lb_ref[...])


def _moe(l, plan, x1, mod, ln_g, ln_b, wg, wu, wd, alpha, tb, tiles_per_seq):
    h2p, pos, wts, meta = plan
    n, d = x1.shape
    f = wg.shape[-1]
    rt = MOE_ROW_TILE
    row = lambda i, e: (i, 0)
    vec = pl.BlockSpec((None, 1, d), lambda i, e: (l, 0, 0))
    smem = lambda w: pl.BlockSpec((None, 1, w), lambda i, e: (i, 0, 0), memory_space=pltpu.SMEM)
    return pl.pallas_call(
        functools.partial(_moe_kernel, alpha=alpha, rt=rt),
        out_shape=jax.ShapeDtypeStruct((n, d), F32),
        grid=(n // tb, N_EXPERTS),
        in_specs=[
            smem(2 * tb), smem(2 * tb), smem(LANES),
            pl.BlockSpec((tb, d // 2), row),
            pl.BlockSpec((tb, d), row),
            _mod_spec(l, 5, d, tb, tiles_per_seq),
            vec, vec,
            pl.BlockSpec((None, None, d, f), lambda i, e: (l, e, 0, 0)),
            pl.BlockSpec((None, None, d, f), lambda i, e: (l, e, 0, 0)),
            pl.BlockSpec((None, None, f, d), lambda i, e: (l, e, 0, 0)),
        ],
        out_specs=pl.BlockSpec((tb, d), row),
        scratch_shapes=[pltpu.VMEM((2 * tb + N_EXPERTS * rt, d // 2), jnp.uint32)],
        compiler_params=_cparams(("arbitrary", "arbitrary")),
        name="moe_routed",
    )(pos, wts, meta, h2p, x1, mod, ln_g, ln_b, wg, wu, wd)


def _rope_tables(pos):
    half = QK_ROPE // 2
    inv = jnp.power(ROPE_THETA, -jnp.arange(half, dtype=F32) * 2.0 / QK_ROPE)
    ang = pos.astype(F32)[:, None] * inv[None, :]
    cos, sin = jnp.cos(ang), jnp.sin(ang)
    pad = jnp.zeros((pos.shape[0], ROPE_PAD - QK_ROPE), F32)
    return (jnp.concatenate([cos, cos, pad], axis=1),
            jnp.concatenate([-sin, sin, pad], axis=1))


def _swap_halves(w):
    half = QK_ROPE // 2
    return jnp.concatenate([w[..., half:], w[..., :half]], axis=-1)


def _pad_last(w, width):
    return jnp.pad(w, [(0, 0)] * (w.ndim - 1) + [(0, width - w.shape[-1])])


def kernel(x_prompt, x_sample, cache_latent, cache_krope, state_conv, page_table, c_prompt, c_sample,
           w_in, q_norm_g, w_uq, kv_norm_g, w_uk, w_uv, conv_w, w_o, ln1_g, ln1_b, ln2_g, ln2_b,
           w_ada, b_ada, w_router, router_bias, w_gate, w_up, w_down):
    batch, seq, d = x_prompt.shape
    dec_b, dec_seq, _ = x_sample.shape
    depth = w_in.shape[0]
    n_pages = page_table.shape[1]
    past_len = n_pages * PAGE_SIZE
    alpha = (2 * depth) ** 0.25
    n_p, n_s = batch * seq, dec_b * dec_seq

    s0, s1, s2, s3, s4 = (Q_LORA, Q_LORA + KV_LORA, Q_LORA + KV_LORA + QK_ROPE,
                          Q_LORA + KV_LORA + QK_ROPE + CONV_CH,
                          Q_LORA + KV_LORA + QK_ROPE + 2 * CONV_CH)
    w_kr = w_in[:, :, s1:s2]
    w_in_r = jnp.concatenate([
        w_in[:, :, :s1], w_in[:, :, s2:],
        _pad_last(w_kr, ROPE_PAD), _pad_last(_swap_halves(w_kr), ROPE_PAD)], axis=-1).astype(BF16)

    w_uq_t = jnp.transpose(w_uq, (0, 2, 1, 3))
    w_uk_t = jnp.transpose(w_uk, (0, 2, 1, 3))
    wuv = jnp.transpose(w_uv, (0, 2, 1, 3)).astype(BF16)
    wuvt = jnp.transpose(w_uv, (0, 2, 3, 1)).astype(BF16)
    kpe_cache_t = jnp.swapaxes(cache_krope, 2, 3)
    qg = q_norm_g[:, :, None]
    w_lat = _fuse_q_weights(w_uq_t[..., :QK_NOPE], w_uk_t, qg)
    w_pe = w_uq_t[..., QK_NOPE:] * qg[:, None] * Q_SCALE
    wa = jnp.concatenate([w_lat, _pad_last(w_pe, ROPE_PAD)], axis=-1).astype(BF16)
    wb = _pad_last(_swap_halves(w_pe), ROPE_PAD).astype(BF16)

    w_o_b = w_o.astype(BF16)
    wg_b, wu_b, wd_b = w_gate.astype(BF16), w_up.astype(BF16), w_down.astype(BF16)
    perm = jnp.arange(N_EXPERTS).reshape(N_GROUPS, EXP_PER_GROUP).T.reshape(-1)
    wr_f = w_router.T[perm]
    wr_hi = wr_f.astype(BF16)
    wr = jnp.concatenate([wr_hi, (wr_f - wr_hi.astype(F32)).astype(BF16)], axis=0)
    rb = router_bias[perm].reshape(N_EXPERTS, 1)
    tb_p, tb_s = min(1024, n_p), min(1024, n_s)
    tri_p = jnp.triu(jnp.ones((tb_p, tb_p), BF16), k=1)
    tri_s = tri_p if tb_s == tb_p else jnp.triu(jnp.ones((tb_s, tb_s), BF16), k=1)
    mmat = jnp.concatenate([jnp.triu(jnp.ones((N_EXPERTS, N_EXPERTS), BF16), k=1),
                            jnp.eye(N_EXPERTS, dtype=BF16),
                            jnp.zeros((N_EXPERTS, LANES - 2 * N_EXPERTS), BF16)], axis=1)

    kv_g = kv_norm_g.reshape(depth, 1, KV_LORA)
    ln1g, ln1b = ln1_g.reshape(depth, 1, d), ln1_b.reshape(depth, 1, d)
    ln2g, ln2b = ln2_g.reshape(depth, 1, d), ln2_b.reshape(depth, 1, d)

    cos_p, sin_p = _rope_tables(jnp.arange(seq))
    cos_s, sin_s = _rope_tables(past_len + jnp.arange(n_s) % dec_seq)

    c_all = jnp.concatenate([jnp.repeat(c_sample, dec_seq, axis=0), c_prompt], axis=0)
    mod = _ada_mod(c_all, w_ada, b_ada)
    mod_p = mod[:, n_s:].reshape(depth, batch, 1, 6 * d)

    tm_p = min(512, seq)
    tps = seq // tm_p
    tq = min(512, seq)
    tk = min(512, seq)
    tm_s = min(256, n_s)

    xp = x_prompt.reshape(n_p, d)
    xs = x_sample.reshape(n_s, d)
    outs = [[] for _ in range(6)]
    for l in range(depth):
        qn, ckv, kpe, kcat, latt, oconv, cst = _premix_prompt(
            l, xp, mod_p, w_in_r, kv_g, conv_w, cos_p, sin_p, batch, seq, tm_p)
        o_att = _attn_prompt(l, qn, kcat, latt, wa, wb, cos_p, sin_p, wuvt, batch, seq, tq, tk)
        x1 = _postmix(l, xp, o_att, oconv, w_o_b, mod_p, ln1g, ln1b, alpha, tm_p, tps)
        plan = _plan(l, x1, mod_p, wr, rb, tri_p, mmat, tb_p, seq // tb_p)
        xp = _moe(l, plan, x1, mod_p, ln2g, ln2b, wg_b, wu_b, wd_b, alpha, tb_p, seq // tb_p)
        outs[0].append(ckv.reshape(batch, seq, KV_LORA))
        outs[1].append(kpe.reshape(batch, seq, QK_ROPE))
        outs[2].append(cst)
        st = state_conv[l]
        pad_n = dec_seq - 1
        e1 = jnp.pad(st[:, 1:2], ((0, 0), (0, pad_n), (0, 0))).reshape(n_s, CONV_CH)
        e2 = jnp.pad(st, ((0, 0), (0, dec_seq - 2), (0, 0))).reshape(n_s, CONV_CH)
        qs, ckv_s, kpe_s, kcat_s, oconv_s, u_s = _premix_sample(
            l, xs, mod, w_in_r, kv_g, conv_w, cos_s, sin_s, e1, e2, wa, wb, dec_seq, tm_s)
        o_lat = _attn_sample(l, page_table, qs, kcat_s, cache_latent, kpe_cache_t, dec_seq)
        o_att_s = _uv_sample(l, o_lat, wuv, dec_seq, tm_s // dec_seq)
        x1s = _postmix(l, xs, o_att_s, oconv_s, w_o_b, mod, ln1g, ln1b, alpha, tm_s, None)
        plan_s = _plan(l, x1s, mod, wr, rb, tri_s, mmat, tb_s, None)
        xs = _moe(l, plan_s, x1s, mod, ln2g, ln2b, wg_b, wu_b, wd_b, alpha, tb_s, None)
        outs[3].append(ckv_s.reshape(dec_b, dec_seq, KV_LORA))
        outs[4].append(kpe_s.reshape(dec_b, dec_seq, QK_ROPE))
        outs[5].append(u_s.reshape(dec_b, dec_seq, CONV_CH)[:, dec_seq - (CONV_K - 1):])
    return (xp.reshape(batch, seq, d), xs.reshape(dec_b, dec_seq, d),
            jnp.stack(outs[0]), jnp.stack(outs[1]), jnp.stack(outs[2]),
            jnp.stack(outs[3]), jnp.stack(outs[4]), jnp.stack(outs[5]))
```
